```python
import math
import jax
import jax.numpy as jnp
from jax import lax
import numpy as np

D_MODEL = 1024
BATCH = 8
SEQ = 4096
DEPTH = 1
DEC_BATCH = 128
DEC_SEQ = 1
PAST_LEN = 8192
PAGE_SIZE = 128

N_HEADS_A = 8
HEAD_DIM_A = 64
WIDTH_A = N_HEADS_A * HEAD_DIM_A
MOBA_BLOCK = 256
MOBA_TOPK = 3
Q_CHUNK = 32
N_HEADS_B = 4
KEY_DIM_B = 128
VAL_DIM_B = 128
WIDTH_B = N_HEADS_B * VAL_DIM_B
HGRN_CHUNK = 32
NORM_EPS = 1e-6

kernel_name = 'moba_hgrn2_parallel_hybrid_step'


def _in_splits():
    wk = N_HEADS_B * KEY_DIM_B
    return (WIDTH_A, WIDTH_A, WIDTH_A, WIDTH_A, wk, wk, WIDTH_B, WIDTH_B, D_MODEL, D_MODEL)


def _rmsnorm(x, g):
    xf = x.astype(jnp.float32)
    y = xf * lax.rsqrt(jnp.mean(xf * xf, axis=-1, keepdims=True) + NORM_EPS)
    return (y * g.astype(jnp.float32)).astype(x.dtype)


def _alibi_slopes(n):
    return jnp.asarray([2.0 ** (-8.0 * (i + 1) / n) for i in range(n)], dtype=jnp.float32)


def _heads(t, n_heads):
    b, L, w = t.shape
    return t.reshape(b, L, n_heads, w // n_heads).transpose(0, 2, 1, 3)


def _merge(t):
    b, h, L, d = t.shape
    return t.transpose(0, 2, 1, 3).reshape(b, L, h * d)


def _attend(q, tq, parts, slopes):
    scale = q.shape[-1] ** -0.5
    qf = q.astype(jnp.float32)
    logits = []
    for k, v, kpos, valid in parts:
        spec = 'bhqd,bhqld->bhql' if k.ndim == 5 else 'bhqd,bhld->bhql'
        sc = jnp.einsum(spec, qf, k.astype(jnp.float32)) * scale
        dist = jnp.abs(tq[:, None] - kpos).astype(jnp.float32)
        lg = sc - slopes[None, :, None, None] * dist
        if valid is not None:
            lg = jnp.where(valid, lg, -jnp.inf)
        logits.append(lg)
    p = jax.nn.softmax(jnp.concatenate(logits, axis=-1), axis=-1)
    outs = []
    off = 0
    for (k, v, kpos, valid), lg in zip(parts, logits):
        n = lg.shape[-1]
        spec = 'bhql,bhqld->bhqd' if v.ndim == 5 else 'bhql,bhld->bhqd'
        outs.append(jnp.einsum(spec, p[..., off:off + n], v.astype(jnp.float32)))
        off += n
    return sum(outs[1:], outs[0]).astype(q.dtype)


def _moba_prompt(q, k, v, slopes):
    b, h, s, dh = q.shape
    nb = -(-s // MOBA_BLOCK)
    s_pad = nb * MOBA_BLOCK
    padw = ((0, 0), (0, 0), (0, s_pad - s), (0, 0))
    q, k, v = (jnp.pad(a, padw) for a in (q, k, v))
    kb = k.reshape(b, h, nb, MOBA_BLOCK, dh)
    vb = v.reshape(b, h, nb, MOBA_BLOCK, dh)
    kmean = jnp.mean(kb.astype(jnp.float32), axis=3)
    qblk = jnp.arange(s_pad) // MOBA_BLOCK
    gate = jnp.einsum('bhsd,bhnd->bhsn', q.astype(jnp.float32), kmean)
    gate = jnp.where(jnp.arange(nb)[None, :] < qblk[:, None], gate, -jnp.inf)
    n_sel = min(MOBA_TOPK, nb)
    _, sel = lax.top_k(gate, n_sel)
    bi = jnp.arange(b)[:, None, None, None]
    hi = jnp.arange(h)[None, :, None, None]

    def chunk_fn(ci):
        t0 = ci * Q_CHUNK
        qc = lax.dynamic_slice_in_dim(q, t0, Q_CHUNK, axis=2)
        selc = lax.dynamic_slice_in_dim(sel, t0, Q_CHUNK, axis=2)
        tq = t0 + jnp.arange(Q_CHUNK)
        jq = t0 // MOBA_BLOCK
        k_own = lax.dynamic_index_in_dim(kb, jq, axis=2, keepdims=False)
        v_own = lax.dynamic_index_in_dim(vb, jq, axis=2, keepdims=False)
        own_pos = jq * MOBA_BLOCK + jnp.arange(MOBA_BLOCK)
        own_valid = own_pos[None, :] <= tq[:, None]
        k_sel = kb[bi, hi, selc].reshape(b, h, Q_CHUNK, n_sel * MOBA_BLOCK, dh)
        v_sel = vb[bi, hi, selc].reshape(b, h, Q_CHUNK, n_sel * MOBA_BLOCK, dh)
        sel_pos = (selc[..., None] * MOBA_BLOCK + jnp.arange(MOBA_BLOCK)).reshape(b, h, Q_CHUNK, -1)
        sel_valid = jnp.broadcast_to((selc < jq)[..., None], selc.shape + (MOBA_BLOCK,)).reshape(b, h, Q_CHUNK, -1)
        parts = [(k_sel, v_sel, sel_pos, sel_valid), (k_own, v_own, own_pos, own_valid)]
        return _attend(qc, tq, parts, slopes)

    out = lax.map(chunk_fn, jnp.arange(s_pad // Q_CHUNK))
    out = out.transpose(1, 2, 0, 3, 4).reshape(b, h, s_pad, dh)
    return out[:, :, :s]


def _moba_sample(q, k_new, v_new, cache_k, cache_v, page_table, slopes):
    db, h, ds, dh = q.shape
    n_pages = page_table.shape[1]
    past = n_pages * PAGE_SIZE
    ppb = MOBA_BLOCK // PAGE_SIZE
    n_full = past // MOBA_BLOCK
    own_start = n_full * MOBA_BLOCK
    tq = past + jnp.arange(ds)
    parts = []
    if n_full > 0:
        page_sum = jnp.sum(cache_k.astype(jnp.float32), axis=2)
        blk = page_sum[page_table[:, :n_full * ppb]]
        kmean = jnp.sum(blk.reshape(db, n_full, ppb, h, dh), axis=2) / MOBA_BLOCK
        gate = jnp.einsum('bhqd,bnhd->bhqn', q.astype(jnp.float32), kmean)
        n_sel = min(MOBA_TOPK, n_full)
        _, sel = lax.top_k(gate, n_sel)
        lpages = sel[..., None] * ppb + jnp.arange(ppb)
        phys = page_table[jnp.arange(db)[:, None, None, None, None], lpages]
        hi = jnp.arange(h)[None, :, None, None, None]
        k_sel = cache_k[phys, hi].reshape(db, h, ds, n_sel * MOBA_BLOCK, dh)
        v_sel = cache_v[phys, hi].reshape(db, h, ds, n_sel * MOBA_BLOCK, dh)
        sel_pos = (lpages[..., None] * PAGE_SIZE + jnp.arange(PAGE_SIZE)).reshape(db, h, ds, -1)
        parts.append((k_sel, v_sel, sel_pos, None))
    if past > own_start:
        own_pages = page_table[:, n_full * ppb:]
        k_own = cache_k[own_pages].transpose(0, 2, 1, 3, 4).reshape(db, h, -1, dh)
        v_own = cache_v[own_pages].transpose(0, 2, 1, 3, 4).reshape(db, h, -1, dh)
        own_pos = own_start + jnp.arange(past - own_start)
        parts.append((k_own, v_own, own_pos, None))
    parts.append((k_new, v_new, tq, tq[None, :] <= tq[:, None]))
    return _attend(q, tq, parts, slopes)


def _hgrn_recurrence(q, k, v, log_f, s0):
    out_dtype, st_dtype = v.dtype, s0.dtype
    q, k, v, log_f = (a.astype(jnp.float32) for a in (q, k, v, log_f))
    b, h, L, dk = q.shape
    dv = v.shape[-1]
    c = min(HGRN_CHUNK, L)
    n = -(-L // c)
    padw = ((0, 0), (0, 0), (0, n * c - L), (0, 0))
    q, k, v, log_f = (jnp.pad(a, padw) for a in (q, k, v, log_f))

    def to_chunks(a):
        return a.reshape(b, h, n, c, a.shape[-1]).transpose(2, 0, 1, 3, 4)

    causal = jnp.tril(jnp.ones((c, c), dtype=bool))

    def step(s, inp):
        qc, kc, vc, gc = inp
        cum = jnp.cumsum(gc, axis=2)
        total = cum[:, :, -1:, :]
        q_dec = qc * jnp.exp(cum)
        a = jnp.einsum('bhtk,bhsk->bhts', q_dec, kc * jnp.exp(-cum))
        a = jnp.where(causal, a, 0.0)
        o = jnp.einsum('bhts,bhsv->bhtv', a, vc) + jnp.einsum('bhtk,bhkv->bhtv', q_dec, s)
        s_new = jnp.exp(total)[:, :, 0, :, None] * s + jnp.einsum('bhsk,bhsv->bhkv', kc * jnp.exp(total - cum), vc)
        return s_new, o

    s_fin, o = lax.scan(step, s0.astype(jnp.float32), (to_chunks(q), to_chunks(k), to_chunks(v), to_chunks(log_f)))
    o = o.transpose(1, 2, 0, 3, 4).reshape(b, h, n * c, dv)[:, :, :L]
    return o.astype(out_dtype), s_fin.astype(st_dtype)


def _layer(x, attn_fn, s0, layer, norm_in_g, w_in, hgrn_lb_logits, hgrn_norm_g, w_branch_a, w_branch_b, w_out):
    b, L, _ = x.shape
    hn = _rmsnorm(x, norm_in_g)
    points = [int(p) for p in np.cumsum(_in_splits())[:-1]]
    q_a, k_a, v_a, z_a, q_b, f_b, i_b, z_b, g_a, g_b = jnp.split(hn @ w_in, points, axis=-1)
    qa, ka, va = _heads(q_a, N_HEADS_A), _heads(k_a, N_HEADS_A), _heads(v_a, N_HEADS_A)
    o_a = _merge(attn_fn(qa, ka, va)) * jax.nn.silu(z_a)
    y_a = o_a @ w_branch_a
    lb = jnp.cumsum(jax.nn.softmax(hgrn_lb_logits.astype(jnp.float32), axis=0), axis=0)[layer]
    f = lb + (1.0 - lb) * jax.nn.sigmoid(f_b.astype(jnp.float32))
    o_b, s_fin = _hgrn_recurrence(_heads(q_b, N_HEADS_B), _heads(1.0 - f, N_HEADS_B),
                                  _heads(i_b, N_HEADS_B), _heads(jnp.log(f), N_HEADS_B), s0)
    o_b = _rmsnorm(o_b.transpose(0, 2, 1, 3), hgrn_norm_g.reshape(N_HEADS_B, VAL_DIM_B)).reshape(b, L, WIDTH_B)
    y_b = (o_b * jax.nn.silu(z_b)) @ w_branch_b
    mixed = jax.nn.sigmoid(g_a) * y_a + jax.nn.sigmoid(g_b) * y_b
    return x + mixed @ w_out, ka, va, s_fin


def setup_inputs(seed: int = 0) -> dict:
    key = jax.random.key(seed)
    ks = jax.random.split(key, 16)
    n_pages = PAST_LEN // PAGE_SIZE
    n_pool = (DEC_BATCH * n_pages * 5) // 4
    in_w = sum(_in_splits())
    nrm = jax.random.normal
    f32 = jnp.float32
    x_prompt = nrm(ks[0], (BATCH, SEQ, D_MODEL), f32)
    x_sample = nrm(ks[1], (DEC_BATCH, DEC_SEQ, D_MODEL), f32)
    cache_k = nrm(ks[2], (DEPTH, n_pool, N_HEADS_A, PAGE_SIZE, HEAD_DIM_A), f32)
    cache_v = nrm(ks[3], (DEPTH, n_pool, N_HEADS_A, PAGE_SIZE, HEAD_DIM_A), f32)
    state_hgrn = 0.5 * nrm(ks[4], (DEPTH, DEC_BATCH, N_HEADS_B, KEY_DIM_B, VAL_DIM_B), f32)
    perm = jax.random.permutation(ks[5], n_pool)[:DEC_BATCH * n_pages]
    page_table = perm.reshape(DEC_BATCH, n_pages).astype(jnp.int32)
    norm_in_g = 1.0 + 0.02 * nrm(ks[6], (DEPTH, D_MODEL), f32)
    w_in = nrm(ks[7], (DEPTH, D_MODEL, in_w), f32) * D_MODEL ** -0.5
    hgrn_lb_logits = 0.1 * nrm(ks[8], (DEPTH + 1, N_HEADS_B * KEY_DIM_B), f32)
    hgrn_norm_g = 1.0 + 0.02 * nrm(ks[9], (DEPTH, WIDTH_B), f32)
    w_branch_a = nrm(ks[10], (DEPTH, WIDTH_A, D_MODEL), f32) * WIDTH_A ** -0.5
    w_branch_b = nrm(ks[11], (DEPTH, WIDTH_B, D_MODEL), f32) * WIDTH_B ** -0.5
    w_out = nrm(ks[12], (DEPTH, D_MODEL, D_MODEL), f32) * D_MODEL ** -0.5
    final_norm_g = 1.0 + 0.02 * nrm(ks[13], (D_MODEL,), f32)
    return {'x_prompt': x_prompt, 'x_sample': x_sample, 'cache_k': cache_k, 'cache_v': cache_v,
            'state_hgrn': state_hgrn, 'page_table': page_table, 'norm_in_g': norm_in_g, 'w_in': w_in,
            'hgrn_lb_logits': hgrn_lb_logits, 'hgrn_norm_g': hgrn_norm_g, 'w_branch_a': w_branch_a,
            'w_branch_b': w_branch_b, 'w_out': w_out, 'final_norm_g': final_norm_g}


def reference(x_prompt, x_sample, cache_k, cache_v, state_hgrn, page_table, norm_in_g, w_in,
              hgrn_lb_logits, hgrn_norm_g, w_branch_a, w_branch_b, w_out, final_norm_g):
    slopes = _alibi_slopes(N_HEADS_A)
    xp, xs = x_prompt, x_sample
    kp_l, vp_l, sp_l, ks_l, vs_l, ss_l = [], [], [], [], [], []
    for layer in range(DEPTH):
        wl = (norm_in_g[layer], w_in[layer], hgrn_lb_logits, hgrn_norm_g[layer],
              w_branch_a[layer], w_branch_b[layer], w_out[layer])
        s0p = jnp.zeros((xp.shape[0], N_HEADS_B, KEY_DIM_B, VAL_DIM_B), xp.dtype)
        xp, kp, vp, sp = _layer(xp, lambda q, k, v: _moba_prompt(q, k, v, slopes), s0p, layer, *wl)
        b, h, s, dh = kp.shape
        kp_l.append(kp.reshape(b, h, s // PAGE_SIZE, PAGE_SIZE, dh).transpose(0, 2, 1, 3, 4))
        vp_l.append(vp.reshape(b, h, s // PAGE_SIZE, PAGE_SIZE, dh).transpose(0, 2, 1, 3, 4))
        sp_l.append(sp)
        ck, cv = cache_k[layer], cache_v[layer]
        xs, kn, vn, sn = _layer(
            xs, lambda q, k, v, ck=ck, cv=cv: _moba_sample(q, k, v, ck, cv, page_table, slopes),
            state_hgrn[layer], layer, *wl)
        ks_l.append(kn)
        vs_l.append(vn)
        ss_l.append(sn)
    y_prompt = _rmsnorm(xp, final_norm_g)
    y_sample = _rmsnorm(xs, final_norm_g)
    return (y_prompt, y_sample, jnp.stack(kp_l), jnp.stack(vp_l), jnp.stack(sp_l),
            jnp.stack(ks_l), jnp.stack(vs_l), jnp.stack(ss_l))
```

```python
import functools

import numpy as np
import jax
import jax.numpy as jnp
from jax import lax
from jax.experimental import pallas as pl
from jax.experimental.pallas import tpu as pltpu

N_HEADS_A = 8
HEAD_DIM_A = 64
WIDTH_A = N_HEADS_A * HEAD_DIM_A
MOBA_BLOCK = 256
MOBA_TOPK = 3
N_HEADS_B = 4
KEY_DIM_B = 128
VAL_DIM_B = 128
WIDTH_B = N_HEADS_B * VAL_DIM_B
HGRN_CHUNK = 32
NORM_EPS = 1e-6

LANES = 128
MASK_LANES = LANES
MAX_BLOCKS = MASK_LANES // N_HEADS_A
NEG = -(2.0 ** 30)
SLOPE_LANE = HEAD_DIM_A
SLOPE_PARTS = 3
VMEM_LIMIT = 48 * 1024 * 1024

F32 = jnp.float32
BF = jnp.bfloat16
NT = (((1,), (1,)), ((), ()))
TN = (((0,), (0,)), ((), ()))


def _alibi_slopes(n):
    return [2.0 ** (-8.0 * (i + 1) / n) for i in range(n)]


def _rms(x, g):
    ms = jnp.mean(x * x, axis=-1, keepdims=True)
    return x * lax.rsqrt(ms + NORM_EPS) * g


def _split2(a):
    hi = a.astype(BF)
    lo = (a - hi.astype(F32)).astype(BF)
    return hi, lo


def _split3(a):
    hi = a.astype(BF)
    r = a - hi.astype(F32)
    mid = r.astype(BF)
    lo = (r - mid.astype(F32)).astype(BF)
    return hi, mid, lo


def _dot(a, b):
    return jnp.dot(a, b, preferred_element_type=F32)


def _dot_nt(a, b):
    return lax.dot_general(a, b, NT, preferred_element_type=F32)


def _dot3_nt(a, b):
    ah, al = _split2(a)
    bh, bl = _split2(b)
    return _dot_nt(ah, bh) + _dot_nt(ah, bl) + _dot_nt(al, bh)


def _sigmoid(x):
    return 1.0 / (1.0 + jnp.exp(-x))


def _silu(x):
    return x * _sigmoid(x)


def _params(*sem):
    return pltpu.CompilerParams(dimension_semantics=sem, vmem_limit_bytes=VMEM_LIMIT)


def _allreduce_blocks(x, op):
    s = N_HEADS_A
    while s < MASK_LANES:
        x = op(x, pltpu.roll(x, s, axis=1))
        s *= 2
    return x


def _topk_blocks(gate, valid, blkf):
    x = jnp.where(valid, gate, -jnp.inf)
    sel = jnp.zeros(gate.shape, F32)
    for _ in range(MOBA_TOPK):
        m = _allreduce_blocks(x, jnp.maximum)
        first = _allreduce_blocks(jnp.where(x == m, blkf, float(MASK_LANES)), jnp.minimum)
        pick = blkf == first
        sel = jnp.where(pick, 1.0, sel)
        x = jnp.where(pick, -jnp.inf, x)
    return jnp.where(valid, sel, 0.0)


def _proj_prompt_kernel(x_ref, g_ref, w_ref, qext_ref, qh_ref, kh_ref, mb_ref, vt_ref,
                        kpg_ref, vpg_ref, qb_ref, fb_ref, ib_ref, km_ref):
    i = pl.program_id(1)
    t = x_ref.shape[1]
    page = kpg_ref.shape[4]
    hn = _rms(x_ref[0], g_ref[...]).astype(BF)

    def proj(c):
        return _dot(hn, w_ref[:, c * WIDTH_A:(c + 1) * WIDTH_A])

    q = proj(0)
    k = proj(1)
    v = proj(2)
    qb_ref[0] = proj(3)
    fb_ref[0] = proj(4)
    ib_ref[0] = proj(5)

    lane = lax.broadcasted_iota(jnp.int32, (t, LANES), 1)
    row = lax.broadcasted_iota(jnp.int32, (t, LANES), 0)
    low = lane < HEAD_DIM_A
    kext = jnp.where((lane >= SLOPE_LANE) & (lane < SLOPE_LANE + SLOPE_PARTS), row.astype(F32), 0.0)
    scale = HEAD_DIM_A ** -0.5
    for hp in range(N_HEADS_A // 2):
        cols = slice(hp * LANES, (hp + 1) * LANES)
        qp = q[:, cols] * scale
        kp = k[:, cols]
        qpr = pltpu.roll(qp, HEAD_DIM_A, axis=1)
        kpr = pltpu.roll(kp, HEAD_DIM_A, axis=1)
        qh_ref[0, 2 * hp] = jnp.where(low, qp, qext_ref[2 * hp:2 * hp + 1, :]).astype(BF)
        qh_ref[0, 2 * hp + 1] = jnp.where(low, qpr, qext_ref[2 * hp + 1:2 * hp + 2, :]).astype(BF)
        kh_ref[0, 2 * hp, 0] = jnp.where(low, kp, kext).astype(BF)
        kh_ref[0, 2 * hp + 1, 0] = jnp.where(low, kpr, kext).astype(BF)

    kt = k.T
    vt = v.T
    for h in range(N_HEADS_A):
        feat = slice(h * HEAD_DIM_A, (h + 1) * HEAD_DIM_A)
        vt_ref[0, h, 0] = vt[feat, :].astype(BF)
        for p in range(t // page):
            toks = slice(p * page, (p + 1) * page)
            kpg_ref[0, p, h] = kt[feat, toks]
            vpg_ref[0, p, h] = vt[feat, toks]

    @pl.when(i == 0)
    def _():
        km_ref[...] = jnp.zeros(km_ref.shape, F32)

    gate = _dot3_nt(q, km_ref[...])
    blk = lane // N_HEADS_A
    sel = _topk_blocks(gate, blk < i, blk.astype(F32))
    mb_ref[0] = jnp.where((sel > 0.0) | (blk == i), 0.0, NEG).astype(BF)

    kmean = jnp.sum(k, axis=0, keepdims=True) / t
    hrow = lax.broadcasted_iota(jnp.int32, (N_HEADS_A, WIDTH_A), 0)
    hcol = lax.broadcasted_iota(jnp.int32, (N_HEADS_A, WIDTH_A), 1) // HEAD_DIM_A
    km_ref[pl.ds(pl.multiple_of(i * N_HEADS_A, N_HEADS_A), N_HEADS_A), :] = jnp.where(
        hrow == hcol, jnp.broadcast_to(kmean, (N_HEADS_A, WIDTH_A)), 0.0)


def _proj_prompt(x, g_in, w1, qext, page):
    b, s, d = x.shape
    t = MOBA_BLOCK
    nblk = s // t
    nproj = w1.shape[1]
    out_shape = (
        jax.ShapeDtypeStruct((b, N_HEADS_A, s, LANES), BF),
        jax.ShapeDtypeStruct((b, N_HEADS_A, nblk, t, LANES), BF),
        jax.ShapeDtypeStruct((b, s, MASK_LANES), BF),
        jax.ShapeDtypeStruct((b, N_HEADS_A, nblk, HEAD_DIM_A, t), BF),
        jax.ShapeDtypeStruct((b, s // page, N_HEADS_A, HEAD_DIM_A, page), F32),
        jax.ShapeDtypeStruct((b, s // page, N_HEADS_A, HEAD_DIM_A, page), F32),
        jax.ShapeDtypeStruct((b, s, WIDTH_B), F32),
        jax.ShapeDtypeStruct((b, s, WIDTH_B), F32),
        jax.ShapeDtypeStruct((b, s, WIDTH_B), F32),
    )
    pg = pl.BlockSpec((1, t // page, N_HEADS_A, HEAD_DIM_A, page), lambda bi, i: (bi, i, 0, 0, 0))
    row3 = lambda w: pl.BlockSpec((1, t, w), lambda bi, i: (bi, i, 0))
    return pl.pallas_call(
        _proj_prompt_kernel,
        grid=(b, nblk),
        in_specs=[
            row3(d),
            pl.BlockSpec((1, d), lambda bi, i: (0, 0)),
            pl.BlockSpec((d, nproj), lambda bi, i: (0, 0)),
            pl.BlockSpec((N_HEADS_A, LANES), lambda bi, i: (0, 0)),
        ],
        out_specs=(
            pl.BlockSpec((1, N_HEADS_A, t, LANES), lambda bi, i: (bi, 0, i, 0)),
            pl.BlockSpec((1, N_HEADS_A, 1, t, LANES), lambda bi, i: (bi, 0, i, 0, 0)),
            row3(MASK_LANES),
            pl.BlockSpec((1, N_HEADS_A, 1, HEAD_DIM_A, t), lambda bi, i: (bi, 0, i, 0, 0)),
            pg, pg, row3(WIDTH_B), row3(WIDTH_B), row3(WIDTH_B),
        ),
        out_shape=out_shape,
        scratch_shapes=[pltpu.VMEM((MASK_LANES, WIDTH_A), F32)],
        compiler_params=_params("arbitrary", "arbitrary"),
        name="proj_prompt",
    )(x, g_in, w1, qext)


def _moba_kernel(slopes_ref, qh_ref, mb_ref, kh_ref, vt_ref, oh_ref, o_ref):
    hp = pl.program_id(1)
    qi = pl.program_id(2)
    t = mb_ref.shape[1]
    mb = mb_ref[0]
    krow = lax.broadcasted_iota(jnp.int32, (t, t), 0)
    qcol = lax.broadcasted_iota(jnp.int32, (t, t), 1)
    causal = krow <= qcol
    heads = range(qh_ref.shape[1])
    q_aug = [jnp.concatenate([qh_ref[0, hh], mb], axis=1) for hh in heads]
    slope = [slopes_ref[hp * len(heads) + hh] for hh in heads]

    def scores(hh, j):
        k_aug = jnp.concatenate([kh_ref[0, hh, j], oh_ref[hh, j]], axis=1)
        return _dot_nt(k_aug, q_aug[hh])

    def own_block(hh):
        s = jnp.where(causal, scores(hh, qi), NEG)
        m = jnp.max(s, axis=0, keepdims=True)
        p = jnp.exp(s - m)
        l = jnp.sum(p, axis=0, keepdims=True)
        acc = _dot(vt_ref[0, hh, qi], p.astype(BF))
        return m, l, acc

    def past_block(j, carry):
        out = []
        for hh in heads:
            m, l, acc = carry[hh]
            s = scores(hh, j)
            cj = slope[hh] * ((j - qi) * t).astype(F32)
            m_new = jnp.maximum(m, jnp.max(s, axis=0, keepdims=True) + cj)
            alpha = jnp.exp(m - m_new)
            p = jnp.exp(s - (m_new - cj))
            l = alpha * l + jnp.sum(p, axis=0, keepdims=True)
            acc = alpha * acc + _dot(vt_ref[0, hh, j], p.astype(BF))
            out.append((m_new, l, acc))
        return tuple(out)

    carry = lax.fori_loop(0, qi, past_block, tuple(own_block(hh) for hh in heads))
    o_ref[0] = jnp.concatenate([(acc / l).T for (_, l, acc) in carry], axis=1).astype(o_ref.dtype)


def _moba_prompt(slopes, qh, mb, kh, vt, onehot):
    b, _, s, _ = qh.shape
    t = MOBA_BLOCK
    nblk = s // t
    hpp = LANES // HEAD_DIM_A
    return pl.pallas_call(
        _moba_kernel,
        grid_spec=pltpu.PrefetchScalarGridSpec(
            num_scalar_prefetch=1,
            grid=(b, N_HEADS_A // hpp, nblk),
            in_specs=[
                pl.BlockSpec((1, hpp, t, LANES), lambda bi, hp, qi, sl: (bi, hp, qi, 0)),
                pl.BlockSpec((1, t, MASK_LANES), lambda bi, hp, qi, sl: (bi, qi, 0)),
                pl.BlockSpec((1, hpp, nblk, t, LANES), lambda bi, hp, qi, sl: (bi, hp, 0, 0, 0)),
                pl.BlockSpec((1, hpp, nblk, HEAD_DIM_A, t), lambda bi, hp, qi, sl: (bi, hp, 0, 0, 0)),
                pl.BlockSpec((hpp, nblk, t, MASK_LANES), lambda bi, hp, qi, sl: (hp, 0, 0, 0)),
            ],
            out_specs=pl.BlockSpec((1, t, LANES), lambda bi, hp, qi, sl: (bi, qi, hp)),
        ),
        out_shape=jax.ShapeDtypeStruct((b, s, WIDTH_A), BF),
        compiler_params=_params("arbitrary", "arbitrary", "arbitrary"),
        name="moba_prompt",
    )(slopes, qh, mb, kh, vt, onehot)


def _lower_bound(lb_logits, layer):
    e = jnp.exp(lb_logits - jnp.max(lb_logits, axis=0, keepdims=True))
    return jnp.sum(e[:layer + 1], axis=0, keepdims=True) / jnp.sum(e, axis=0, keepdims=True)


def _hgrn_kernel(layer, lbl_ref, gn_ref, tri_ref, q_ref, f_ref, i_ref, o_ref, sfin_ref, st_ref):
    ti = pl.program_id(2)
    t = q_ref.shape[1]
    c = HGRN_CHUNK

    @pl.when(ti == 0)
    def _():
        st_ref[...] = jnp.zeros(st_ref.shape, F32)

    lb = _lower_bound(lbl_ref[...], layer)
    f = lb + (1.0 - lb) * _sigmoid(f_ref[0])
    k = 1.0 - f
    v = i_ref[0]
    tri = tri_ref[...]
    cum = sum(_dot(tri, part) for part in _split3(jnp.log(f)))
    tot = jnp.concatenate(
        [jnp.broadcast_to(cum[(n + 1) * c - 1:(n + 1) * c], (c, cum.shape[1])) for n in range(t // c)], axis=0)
    qd = (q_ref[0] * jnp.exp(cum)).astype(BF)
    ki = (k * jnp.exp(-cum)).astype(BF)
    kd = (k * jnp.exp(tot - cum)).astype(BF)
    vb = v.astype(BF)
    a = jnp.where(tri > 0, _dot_nt(qd, ki), 0.0).astype(BF)
    o_intra = _dot(a, vb)
    st = st_ref[...]
    outs = []
    for n in range(t // c):
        r = slice(n * c, (n + 1) * c)
        outs.append(o_intra[r] + _dot_nt(qd[r], st.astype(BF)))
        upd = lax.dot_general(vb[r], kd[r], TN, preferred_element_type=F32)
        st = st * jnp.exp(tot[(n + 1) * c - 1:(n + 1) * c]) + upd
    st_ref[...] = st
    o = jnp.concatenate(outs, axis=0)
    o_ref[0] = (_rms(o, gn_ref[...])).astype(o_ref.dtype)

    @pl.when(ti == pl.num_programs(2) - 1)
    def _():
        sfin_ref[0, 0] = st.T


def _chunk_tri(t):
    r = np.arange(t)
    return jnp.asarray((r[:, None] // HGRN_CHUNK == r[None, :] // HGRN_CHUNK) & (r[None, :] <= r[:, None]), BF)


def _hgrn_prompt(layer, lb_logits, gnorm, qb, fb, ib):
    b, s, _ = qb.shape
    t = 256
    tok = pl.BlockSpec((1, t, KEY_DIM_B), lambda bi, h, ti: (bi, ti, h))
    return pl.pallas_call(
        functools.partial(_hgrn_kernel, layer),
        grid=(b, N_HEADS_B, s // t),
        in_specs=[
            pl.BlockSpec((lb_logits.shape[0], KEY_DIM_B), lambda bi, h, ti: (0, h)),
            pl.BlockSpec((1, VAL_DIM_B), lambda bi, h, ti: (0, h)),
            pl.BlockSpec((t, t), lambda bi, h, ti: (0, 0)),
            tok, tok, tok,
        ],
        out_specs=(
            pl.BlockSpec((1, t, VAL_DIM_B), lambda bi, h, ti: (bi, ti, h)),
            pl.BlockSpec((1, 1, KEY_DIM_B, VAL_DIM_B), lambda bi, h, ti: (bi, h, 0, 0)),
        ),
        out_shape=(
            jax.ShapeDtypeStruct((b, s, WIDTH_B), BF),
            jax.ShapeDtypeStruct((b, N_HEADS_B, KEY_DIM_B, VAL_DIM_B), F32),
        ),
        scratch_shapes=[pltpu.VMEM((VAL_DIM_B, KEY_DIM_B), F32)],
        compiler_params=_params("arbitrary", "arbitrary", "arbitrary"),
        name="hgrn_prompt",
    )(lb_logits, gnorm, _chunk_tri(t), qb, fb, ib)


def _out_kernel(x_ref, oa_ref, ob_ref, g_ref, w2_ref, wa_ref, wb_ref, wo_ref, gf_ref, y_ref):
    x = x_ref[...]
    hn = _rms(x, g_ref[...]).astype(BF)
    d = x.shape[1]

    def proj(lo, hi):
        return _dot(hn, w2_ref[:, lo:hi])

    ya = _dot((oa_ref[...].astype(F32) * _silu(proj(0, WIDTH_A))).astype(BF), wa_ref[...])
    yb = _dot((ob_ref[...].astype(F32) * _silu(proj(WIDTH_A, WIDTH_A + WIDTH_B))).astype(BF), wb_ref[...])
    g0 = WIDTH_A + WIDTH_B
    mixed = _sigmoid(proj(g0, g0 + d)) * ya + _sigmoid(proj(g0 + d, g0 + 2 * d)) * yb
    out = x + _dot(mixed.astype(BF), wo_ref[...])
    y_ref[...] = _rms(out, gf_ref[...])


def _out_proj(x2, oa, ob, g_in, w2, wa, wb, wo, g_fin, tm):
    n, d = x2.shape
    rows = lambda w: pl.BlockSpec((tm, w), lambda i: (i, 0))
    full = lambda a: pl.BlockSpec(a.shape, lambda i: (0, 0))
    return pl.pallas_call(
        _out_kernel,
        grid=(n // tm,),
        in_specs=[rows(d), rows(WIDTH_A), rows(WIDTH_B), full(g_in), full(w2), full(wa), full(wb), full(wo),
                  full(g_fin)],
        out_specs=rows(d),
        out_shape=jax.ShapeDtypeStruct((n, d), F32),
        compiler_params=_params("arbitrary"),
        name="out_proj",
    )(x2, oa, ob, g_in, w2, wa, wb, wo, g_fin)


def _proj_sample_kernel(x_ref, g_ref, w_ref, r_ref):
    hn = _rms(x_ref[...], g_ref[...]).astype(BF)
    for c in range(w_ref.shape[1] // WIDTH_A):
        cols = slice(c * WIDTH_A, (c + 1) * WIDTH_A)
        r_ref[:, cols] = _dot(hn, w_ref[:, cols])


def _proj_sample(x2, g_in, w1):
    n, _ = x2.shape
    return pl.pallas_call(
        _proj_sample_kernel,
        out_shape=jax.ShapeDtypeStruct((n, w1.shape[1]), F32),
        compiler_params=pltpu.CompilerParams(vmem_limit_bytes=VMEM_LIMIT),
        name="proj_sample",
    )(x2, g_in, w1)


PAGE_CHUNK = 16


def _pagesum_kernel(pt_ref, ck_ref, out_ref, buf, sem):
    ch = buf.shape[1]
    half = ch // 2
    nchunks = pt_ref.shape[0] // ch

    def page_of(c, i):
        return pt_ref[c * ch + (2 * i if i < half else 2 * (i - half) + 1)]

    def copy(c, slot, i):
        return pltpu.make_async_copy(ck_ref.at[page_of(c, i)], buf.at[slot, i], sem.at[slot])

    def start(c, slot):
        for i in range(ch):
            copy(c, slot, i).start()

    c = pl.program_id(0)
    slot = c % 2

    @pl.when(c == 0)
    def _():
        out_ref[...] = jnp.zeros(out_ref.shape, F32)
        start(0, 0)

    @pl.when(c + 1 < nchunks)
    def _():
        start(c + 1, 1 - slot)

    for i in range(ch):
        copy(c, slot, i).wait()
    lane = lax.broadcasted_iota(jnp.int32, (HEAD_DIM_A, LANES), 1)
    grp = (c * half) // LANES
    base = (c * half) % LANES
    for h in range(N_HEADS_A):
        feat = slice(h * HEAD_DIM_A, (h + 1) * HEAD_DIM_A)
        tile = out_ref[grp, feat, :]
        for i in range(half):
            red = jnp.sum(buf[slot, i, h] + buf[slot, half + i, h], axis=1, keepdims=True)
            tile = jnp.where(lane == base + i, red, tile)
        out_ref[grp, feat, :] = tile


def _page_sum(pt_flat, cache_kt):
    _, h, dh, page = cache_kt.shape
    nblocks = pt_flat.shape[0] // (MOBA_BLOCK // page)
    shape = (pl.cdiv(nblocks, LANES), WIDTH_A, LANES)
    return pl.pallas_call(
        _pagesum_kernel,
        grid_spec=pltpu.PrefetchScalarGridSpec(
            num_scalar_prefetch=1,
            grid=(pt_flat.shape[0] // PAGE_CHUNK,),
            in_specs=[pl.BlockSpec(memory_space=pl.ANY)],
            out_specs=pl.BlockSpec(shape, lambda i, pt: (0, 0, 0)),
            scratch_shapes=[pltpu.VMEM((2, PAGE_CHUNK, h, dh, page), F32), pltpu.SemaphoreType.DMA((2,))],
        ),
        out_shape=jax.ShapeDtypeStruct(shape, F32),
        compiler_params=_params("arbitrary"),
        name="page_sum",
    )(pt_flat, cache_kt)


def _sample_gate_kernel(r_ref, bs_ref, sel_ref):
    dpg = r_ref.shape[1]
    nblk = LANES // dpg
    hrow = lax.broadcasted_iota(jnp.int32, (N_HEADS_A, WIDTH_A), 0)
    hcol = lax.broadcasted_iota(jnp.int32, (N_HEADS_A, WIDTH_A), 1) // HEAD_DIM_A
    lane = lax.broadcasted_iota(jnp.int32, (N_HEADS_A, LANES), 1)
    kh, kl = _split2(bs_ref[0] / MOBA_BLOCK)
    for j in range(dpg):
        qbd = jnp.where(hrow == hcol, jnp.broadcast_to(r_ref[0, j:j + 1, :WIDTH_A], (N_HEADS_A, WIDTH_A)), 0.0)
        qh, ql = _split2(qbd)
        gate = _dot(qh, kh) + _dot(qh, kl) + _dot(ql, kh)
        blk = lane - j * nblk
        x = jnp.where((blk >= 0) & (blk < nblk), gate, -jnp.inf)
        out = jnp.zeros((N_HEADS_A, LANES), jnp.int32)
        for rank in range(MOBA_TOPK):
            m = jnp.max(x, axis=1, keepdims=True)
            first = jnp.min(jnp.where(x == m, blk, LANES), axis=1, keepdims=True)
            out = jnp.where(lane == rank, first, out)
            x = jnp.where(blk == first, -jnp.inf, x)
        sel_ref[j] = out


def _sample_gate(r, bsum_t, nblk):
    ndb = r.shape[0]
    dpg = LANES // nblk
    return pl.pallas_call(
        _sample_gate_kernel,
        grid=(ndb // dpg,),
        in_specs=[pl.BlockSpec((1, dpg, r.shape[1]), lambda i: (i, 0, 0)),
                  pl.BlockSpec((1, WIDTH_A, LANES), lambda i: (i, 0, 0))],
        out_specs=pl.BlockSpec((dpg, N_HEADS_A, LANES), lambda i: (i, 0, 0)),
        out_shape=jax.ShapeDtypeStruct((ndb, N_HEADS_A, LANES), jnp.int32),
        compiler_params=_params("arbitrary"),
        name="sample_gate",
    )(r.reshape(ndb // dpg, dpg, r.shape[1]), bsum_t)


def _sample_attn_kernel(past, slopes, sel_ref, pt_ref, qkv_ref, ck_ref, cv_ref, o_ref, kbuf, vbuf, sem):
    db = pl.program_id(0)
    ndb = pl.num_programs(0)
    npg = pt_ref.shape[0] // ndb
    ppb = kbuf.shape[2] // MOBA_TOPK
    page = kbuf.shape[4]

    def copies(d, slot):
        out = []
        for h in range(N_HEADS_A):
            for tp in range(MOBA_TOPK * ppb):
                blk = sel_ref[(d * N_HEADS_A + h) * MOBA_TOPK + tp // ppb]
                pg = pt_ref[d * npg + blk * ppb + tp % ppb]
                out.append(pltpu.make_async_copy(ck_ref.at[pg, h], kbuf.at[slot, h, tp], sem.at[0, slot]))
                out.append(pltpu.make_async_copy(cv_ref.at[pg, h], vbuf.at[slot, h, tp], sem.at[1, slot]))
        return out

    slot = db % 2

    @pl.when(db == 0)
    def _():
        for cp in copies(0, 0):
            cp.start()

    @pl.when(db + 1 < ndb)
    def _():
        for cp in copies(db + 1, 1 - slot):
            cp.start()

    for cp in copies(db, slot):
        cp.wait()

    scale = HEAD_DIM_A ** -0.5
    tok = lax.broadcasted_iota(jnp.int32, (1, page), 1)
    for h in range(N_HEADS_A):
        feat = slice(h * HEAD_DIM_A, (h + 1) * HEAD_DIM_A)
        qc = qkv_ref[0, feat, :] * scale
        knc = qkv_ref[0, WIDTH_A + h * HEAD_DIM_A:WIDTH_A + (h + 1) * HEAD_DIM_A, :]
        vnc = qkv_ref[0, 2 * WIDTH_A + h * HEAD_DIM_A:2 * WIDTH_A + (h + 1) * HEAD_DIM_A, :]
        lg_new = jnp.sum(qc * knc, axis=0, keepdims=True)
        qb = jnp.broadcast_to(qc, (HEAD_DIM_A, page))
        lgs = []
        for tp in range(MOBA_TOPK * ppb):
            blk = sel_ref[(db * N_HEADS_A + h) * MOBA_TOPK + tp // ppb]
            pos0 = blk * (ppb * page) + (tp % ppb) * page
            dist = (past - pos0 - tok).astype(F32)
            sc = jnp.sum(kbuf[slot, h, tp] * qb, axis=0, keepdims=True)
            lgs.append(sc - slopes[h] * dist)
        m = lg_new
        for lg in lgs:
            m = jnp.maximum(m, jnp.max(lg, axis=1, keepdims=True))
        p_new = jnp.exp(lg_new - m)
        l = p_new
        acc = p_new * vnc
        for tp, lg in enumerate(lgs):
            p = jnp.exp(lg - m)
            l = l + jnp.sum(p, axis=1, keepdims=True)
            acc = acc + jnp.sum(vbuf[slot, h, tp] * p, axis=1, keepdims=True)
        o_ref[0, feat, :] = acc / l


def _sample_attn(past, sel_flat, pt_flat, qkv_col, cache_kt, cache_vt):
    ndb = qkv_col.shape[0]
    _, h, dh, page = cache_kt.shape
    ppb = MOBA_BLOCK // page
    kern = functools.partial(_sample_attn_kernel, past, _alibi_slopes(N_HEADS_A))
    return pl.pallas_call(
        kern,
        grid_spec=pltpu.PrefetchScalarGridSpec(
            num_scalar_prefetch=2,
            grid=(ndb,),
            in_specs=[pl.BlockSpec((1, qkv_col.shape[1], 1), lambda d, s, p: (d, 0, 0)),
                      pl.BlockSpec(memory_space=pl.ANY), pl.BlockSpec(memory_space=pl.ANY)],
            out_specs=pl.BlockSpec((1, WIDTH_A, 1), lambda d, s, p: (d, 0, 0)),
            scratch_shapes=[pltpu.VMEM((2, h, MOBA_TOPK * ppb, dh, page), F32),
                            pltpu.VMEM((2, h, MOBA_TOPK * ppb, dh, page), F32),
                            pltpu.SemaphoreType.DMA((2, 2))],
        ),
        out_shape=jax.ShapeDtypeStruct((ndb, WIDTH_A, 1), F32),
        compiler_params=_params("arbitrary"),
        name="sample_attn",
    )(sel_flat, pt_flat, qkv_col, cache_kt, cache_vt)


def _hgrn_sample_kernel(layer, lbl_ref, gn_ref, r_ref, s_ref, o_ref, sn_ref):
    ndb = r_ref.shape[0]
    kk = KEY_DIM_B
    eye = (lax.broadcasted_iota(jnp.int32, (kk, kk), 0) == lax.broadcasted_iota(jnp.int32, (kk, kk), 1))
    ones = jnp.ones((kk, VAL_DIM_B), BF)

    def column(vec):
        diag = jnp.where(eye, jnp.broadcast_to(vec, (kk, kk)), 0.0)
        return sum(_dot(part, ones) for part in _split3(diag))

    lb_all = _lower_bound(lbl_ref[...], layer)
    q0, f0, i0 = WIDTH_A * 3, WIDTH_A * 3 + WIDTH_B, WIDTH_A * 3 + 2 * WIDTH_B
    for db in range(ndb):
        outs = []
        for h in range(N_HEADS_B):
            cols = slice(h * kk, (h + 1) * kk)
            lb = lb_all[:, cols]
            q = r_ref[db:db + 1, q0 + h * kk:q0 + (h + 1) * kk]
            f = lb + (1.0 - lb) * _sigmoid(r_ref[db:db + 1, f0 + h * kk:f0 + (h + 1) * kk])
            v = r_ref[db:db + 1, i0 + h * VAL_DIM_B:i0 + (h + 1) * VAL_DIM_B]
            fcol = column(f)
            s_new = fcol * s_ref[db, h] + (1.0 - fcol) * v
            sn_ref[db, h] = s_new
            o = jnp.sum(column(q) * s_new, axis=0, keepdims=True)
            outs.append(_rms(o, gn_ref[:, h * VAL_DIM_B:(h + 1) * VAL_DIM_B]))
        o_ref[db:db + 1, :] = jnp.concatenate(outs, axis=1).astype(o_ref.dtype)


def _hgrn_sample(layer, lb_logits, gnorm, r, state):
    ndb = r.shape[0]
    tdb = 8
    full = lambda a: pl.BlockSpec(a.shape, lambda i: (0, 0))
    st = pl.BlockSpec((tdb, N_HEADS_B, KEY_DIM_B, VAL_DIM_B), lambda i: (i, 0, 0, 0))
    return pl.pallas_call(
        functools.partial(_hgrn_sample_kernel, layer),
        grid=(ndb // tdb,),
        in_specs=[full(lb_logits), full(gnorm), pl.BlockSpec((tdb, r.shape[1]), lambda i: (i, 0)), st],
        out_specs=(pl.BlockSpec((tdb, WIDTH_B), lambda i: (i, 0)), st),
        out_shape=(jax.ShapeDtypeStruct((ndb, WIDTH_B), BF), jax.ShapeDtypeStruct(state.shape, F32)),
        compiler_params=_params("arbitrary"),
        name="hgrn_sample",
    )(lb_logits, gnorm, r, state)


def _q_extension():
    ext = np.zeros((N_HEADS_A, LANES), np.float32)
    for h, slope in enumerate(_alibi_slopes(N_HEADS_A)):
        rest = np.float32(slope)
        for part in range(SLOPE_PARTS):
            piece = np.float32(rest.astype(jnp.bfloat16))
            ext[h, SLOPE_LANE + part] = piece
            rest = np.float32(rest - piece)
    return jnp.asarray(ext)


def _block_onehot(nblk, t):
    lane = np.arange(MASK_LANES)[None, None, :]
    tgt = (np.arange(nblk)[None, :, None] * N_HEADS_A + np.arange(N_HEADS_A)[:, None, None])
    oh = (lane == tgt).astype(np.float32)
    return jnp.broadcast_to(jnp.asarray(oh, BF)[:, :, None, :], (N_HEADS_A, nblk, t, MASK_LANES))


def kernel(x_prompt, x_sample, cache_k, cache_v, state_hgrn, page_table, norm_in_g, w_in,
           hgrn_lb_logits, hgrn_norm_g, w_branch_a, w_branch_b, w_out, final_norm_g):
    b, s, d = x_prompt.shape
    ndb, ds, _ = x_sample.shape
    depth, n_pool, _, page, _ = cache_k.shape
    npg = page_table.shape[1]
    past = npg * page
    ppb = MOBA_BLOCK // page
    assert depth == 1 and ds == 1
    assert s % MOBA_BLOCK == 0 and s // MOBA_BLOCK <= MAX_BLOCKS
    assert past % MOBA_BLOCK == 0 and past // MOBA_BLOCK >= MOBA_TOPK
    assert ppb == 2 and npg % PAGE_CHUNK == 0
    nblk_s = npg // ppb
    assert LANES % nblk_s == 0 and ndb % (LANES // nblk_s) == 0
    layer = 0

    wk = N_HEADS_B * KEY_DIM_B
    a3 = 3 * WIDTH_A
    wl = w_in[layer]
    w1 = jnp.concatenate([wl[:, :a3], wl[:, a3 + WIDTH_A:a3 + WIDTH_A + 2 * wk + WIDTH_B]], axis=1).astype(BF)
    zb0 = a3 + WIDTH_A + 2 * wk + WIDTH_B
    w2 = jnp.concatenate([wl[:, a3:a3 + WIDTH_A], wl[:, zb0:]], axis=1).astype(BF)
    wa = w_branch_a[layer].astype(BF)
    wb = w_branch_b[layer].astype(BF)
    wo = w_out[layer].astype(BF)
    g_in = norm_in_g[layer][None, :]
    g_fin = final_norm_g[None, :]
    gnorm = hgrn_norm_g[layer][None, :]
    slopes = jnp.asarray(_alibi_slopes(N_HEADS_A), F32)

    qh, kh, mb, vt, k_pg, v_pg, qb, fb, ib = _proj_prompt(x_prompt, g_in, w1, _q_extension(), page)
    o_a = _moba_prompt(slopes, qh, mb, kh, vt, _block_onehot(s // MOBA_BLOCK, MOBA_BLOCK))
    o_b, s_prompt = _hgrn_prompt(layer, hgrn_lb_logits, gnorm, qb, fb, ib)
    y_prompt = _out_proj(x_prompt.reshape(b * s, d), o_a.reshape(b * s, WIDTH_A), o_b.reshape(b * s, WIDTH_B),
                         g_in, w2, wa, wb, wo, g_fin, 512).reshape(b, s, d)

    xs = x_sample.reshape(ndb, d)
    r = _proj_sample(xs, g_in, w1)
    pt_flat = page_table.reshape(ndb * npg)
    ckt = jnp.swapaxes(cache_k[layer], -1, -2)
    cvt = jnp.swapaxes(cache_v[layer], -1, -2)
    bsum_t = _page_sum(pt_flat, ckt)
    sel = _sample_gate(r, bsum_t, nblk_s)[:, :, :MOBA_TOPK].reshape(ndb * N_HEADS_A * MOBA_TOPK)
    qkv_col = r[:, :3 * WIDTH_A].reshape(ndb, 3 * WIDTH_A, 1)
    oa_s = _sample_attn(past, sel, pt_flat, qkv_col, ckt, cvt).reshape(ndb, WIDTH_A)
    ob_s, s_sample = _hgrn_sample(layer, hgrn_lb_logits, gnorm, r, state_hgrn[layer])
    y_sample = _out_proj(xs, oa_s.astype(BF), ob_s, g_in, w2, wa, wb, wo, g_fin, ndb).reshape(ndb, 1, d)

    k_s = r[:, WIDTH_A:2 * WIDTH_A].reshape(ndb, N_HEADS_A, 1, HEAD_DIM_A)
    v_s = r[:, 2 * WIDTH_A:3 * WIDTH_A].reshape(ndb, N_HEADS_A, 1, HEAD_DIM_A)
    k_p = jnp.swapaxes(k_pg, -1, -2)
    v_p = jnp.swapaxes(v_pg, -1, -2)
    return (y_prompt, y_sample, k_p[None], v_p[None], s_prompt[None], k_s[None], v_s[None], s_sample[None])
```

```python
import functools

import numpy as np
import jax
import jax.numpy as jnp
from jax import lax
from jax.experimental import pallas as pl
from jax.experimental.pallas import tpu as pltpu

N_HEADS_A = 8
HEAD_DIM_A = 64
WIDTH_A = N_HEADS_A * HEAD_DIM_A
MOBA_BLOCK = 256
MOBA_TOPK = 3
N_HEADS_B = 4
KEY_DIM_B = 128
VAL_DIM_B = 128
WIDTH_B = N_HEADS_B * VAL_DIM_B
HGRN_CHUNK = 32
NORM_EPS = 1e-6

LANES = 128
MASK_LANES = LANES
MAX_BLOCKS = MASK_LANES // N_HEADS_A
NEG = -(2.0 ** 30)
SLOPE_LANE = HEAD_DIM_A
SLOPE_PARTS = 3
LOG2E = 1.4426950408889634
BF16_ROWS = 16
V_ROWS = HEAD_DIM_A + BF16_ROWS
MOBA_HEADS = 4
MOBA_GROUP = 2
VMEM_LIMIT = 48 * 1024 * 1024

F32 = jnp.float32
BF = jnp.bfloat16
NT = (((1,), (1,)), ((), ()))
TN = (((0,), (0,)), ((), ()))


def _alibi_slopes(n):
    return [2.0 ** (-8.0 * (i + 1) / n) for i in range(n)]


def _rms(x, g):
    ms = jnp.mean(x * x, axis=-1, keepdims=True)
    return x * lax.rsqrt(ms + NORM_EPS) * g


def _split2(a):
    hi = a.astype(BF)
    lo = (a - hi.astype(F32)).astype(BF)
    return hi, lo


def _split3(a):
    hi = a.astype(BF)
    r = a - hi.astype(F32)
    mid = r.astype(BF)
    lo = (r - mid.astype(F32)).astype(BF)
    return hi, mid, lo


def _dot(a, b):
    return jnp.dot(a, b, preferred_element_type=F32)


def _dot_nt(a, b):
    return lax.dot_general(a, b, NT, preferred_element_type=F32)


def _dot3_nt(a, b):
    ah, al = _split2(a)
    bh, bl = _split2(b)
    return _dot_nt(ah, bh) + _dot_nt(ah, bl) + _dot_nt(al, bh)


def _sigmoid(x):
    return 1.0 / (1.0 + jnp.exp(-x))


def _silu(x):
    return x * _sigmoid(x)


def _params(*sem):
    return pltpu.CompilerParams(dimension_semantics=sem, vmem_limit_bytes=VMEM_LIMIT)


def _topk_blocks(gate, valid):
    x = jnp.where(valid, gate, -jnp.inf)
    blk1 = lax.broadcasted_iota(jnp.int32, (1, MASK_LANES), 1) // N_HEADS_A
    rank = jnp.zeros(gate.shape, F32)
    for kb in range(1, MAX_BLOCKS):
        y = pltpu.roll(x, kb * N_HEADS_A, axis=1)
        lower_index = jnp.where(blk1 >= kb, 1.0, 0.0)
        rank = rank + jnp.where(y > x, 1.0, jnp.where(y == x, lower_index, 0.0))
    return jnp.where(valid, jnp.where(rank < MOBA_TOPK, 1.0, 0.0), 0.0)


def _proj_prompt_kernel(x_ref, g_ref, w_ref, qext_ref, qh_ref, kh_ref, mb_ref, vt_ref,
                        kpg_ref, vpg_ref, qb_ref, fb_ref, ib_ref, km_ref):
    i = pl.program_id(1)
    t = x_ref.shape[1]
    page = kpg_ref.shape[4]
    hn = _rms(x_ref[0], g_ref[...]).astype(BF)

    def proj(c):
        return _dot(hn, w_ref[:, c * WIDTH_A:(c + 1) * WIDTH_A])

    q = proj(0)
    k = proj(1)
    v = proj(2)
    qb_ref[0] = proj(3)
    fb_ref[0] = proj(4)
    ib_ref[0] = proj(5)

    lane = lax.broadcasted_iota(jnp.int32, (t, LANES), 1)
    row = lax.broadcasted_iota(jnp.int32, (t, LANES), 0)
    low = lane < HEAD_DIM_A
    kext = jnp.where((lane >= SLOPE_LANE) & (lane < SLOPE_LANE + SLOPE_PARTS), row.astype(F32), 0.0)
    scale = HEAD_DIM_A ** -0.5 * LOG2E
    for hp in range(N_HEADS_A // 2):
        cols = slice(hp * LANES, (hp + 1) * LANES)
        qp = q[:, cols] * scale
        kp = k[:, cols]
        qpr = pltpu.roll(qp, HEAD_DIM_A, axis=1)
        kpr = pltpu.roll(kp, HEAD_DIM_A, axis=1)
        qh_ref[0, 2 * hp] = jnp.where(low, qp, qext_ref[2 * hp:2 * hp + 1, :]).astype(BF)
        qh_ref[0, 2 * hp + 1] = jnp.where(low, qpr, qext_ref[2 * hp + 1:2 * hp + 2, :]).astype(BF)
        kh_ref[0, 2 * hp, 0] = jnp.where(low, kp, kext).astype(BF)
        kh_ref[0, 2 * hp + 1, 0] = jnp.where(low, kpr, kext).astype(BF)

    kt = k.T
    vt = v.T
    ones_row = (lax.broadcasted_iota(jnp.int32, (BF16_ROWS, t), 0) == 0).astype(F32)
    for h in range(N_HEADS_A):
        feat = slice(h * HEAD_DIM_A, (h + 1) * HEAD_DIM_A)
        vt_ref[0, h, 0] = jnp.concatenate([vt[feat, :], ones_row], axis=0).astype(BF)
        for p in range(t // page):
            toks = slice(p * page, (p + 1) * page)
            kpg_ref[0, p, h] = kt[feat, toks]
            vpg_ref[0, p, h] = vt[feat, toks]

    @pl.when(i == 0)
    def _():
        km_ref[...] = jnp.zeros(km_ref.shape, F32)

    gate = _dot3_nt(q, km_ref[...])
    blk = lane // N_HEADS_A
    sel = _topk_blocks(gate, blk < i)
    mb_ref[0] = jnp.where(sel > 0.0, 0.0, NEG).astype(BF)

    kmean = jnp.sum(k, axis=0, keepdims=True) / t
    hrow = lax.broadcasted_iota(jnp.int32, (N_HEADS_A, WIDTH_A), 0)
    hcol = lax.broadcasted_iota(jnp.int32, (N_HEADS_A, WIDTH_A), 1) // HEAD_DIM_A
    km_ref[pl.ds(pl.multiple_of(i * N_HEADS_A, N_HEADS_A), N_HEADS_A), :] = jnp.where(
        hrow == hcol, jnp.broadcast_to(kmean, (N_HEADS_A, WIDTH_A)), 0.0)


def _proj_prompt(x, g_in, w1, qext, page):
    b, s, d = x.shape
    t = MOBA_BLOCK
    nblk = s // t
    nproj = w1.shape[1]
    out_shape = (
        jax.ShapeDtypeStruct((b, N_HEADS_A, s, LANES), BF),
        jax.ShapeDtypeStruct((b, N_HEADS_A, nblk, t, LANES), BF),
        jax.ShapeDtypeStruct((b, s, MASK_LANES), BF),
        jax.ShapeDtypeStruct((b, N_HEADS_A, nblk, V_ROWS, t), BF),
        jax.ShapeDtypeStruct((b, s // page, N_HEADS_A, HEAD_DIM_A, page), F32),
        jax.ShapeDtypeStruct((b, s // page, N_HEADS_A, HEAD_DIM_A, page), F32),
        jax.ShapeDtypeStruct((b, s, WIDTH_B), F32),
        jax.ShapeDtypeStruct((b, s, WIDTH_B), F32),
        jax.ShapeDtypeStruct((b, s, WIDTH_B), F32),
    )
    pg = pl.BlockSpec((1, t // page, N_HEADS_A, HEAD_DIM_A, page), lambda bi, i: (bi, i, 0, 0, 0))
    row3 = lambda w: pl.BlockSpec((1, t, w), lambda bi, i: (bi, i, 0))
    return pl.pallas_call(
        _proj_prompt_kernel,
        grid=(b, nblk),
        in_specs=[
            row3(d),
            pl.BlockSpec((1, d), lambda bi, i: (0, 0)),
            pl.BlockSpec((d, nproj), lambda bi, i: (0, 0)),
            pl.BlockSpec((N_HEADS_A, LANES), lambda bi, i: (0, 0)),
        ],
        out_specs=(
            pl.BlockSpec((1, N_HEADS_A, t, LANES), lambda bi, i: (bi, 0, i, 0)),
            pl.BlockSpec((1, N_HEADS_A, 1, t, LANES), lambda bi, i: (bi, 0, i, 0, 0)),
            row3(MASK_LANES),
            pl.BlockSpec((1, N_HEADS_A, 1, V_ROWS, t), lambda bi, i: (bi, 0, i, 0, 0)),
            pg, pg, row3(WIDTH_B), row3(WIDTH_B), row3(WIDTH_B),
        ),
        out_shape=out_shape,
        scratch_shapes=[pltpu.VMEM((MASK_LANES, WIDTH_A), F32)],
        compiler_params=_params("arbitrary", "arbitrary"),
        name="proj_prompt",
    )(x, g_in, w1, qext)


def _moba_kernel(slopes_ref, qh_ref, mb_ref, kh_ref, vt_ref, oh_ref, o_ref):
    hg = pl.program_id(1)
    qi = pl.program_id(2)
    t = mb_ref.shape[1]
    grp = MOBA_GROUP
    mb = mb_ref[0]
    krow = lax.broadcasted_iota(jnp.int32, (t, t), 0)
    qcol = lax.broadcasted_iota(jnp.int32, (t, t), 1)
    causal = krow <= qcol
    heads = range(qh_ref.shape[1])
    q_aug = [jnp.concatenate([qh_ref[0, hh], mb], axis=1) for hh in heads]
    slope = [slopes_ref[hg * len(heads) + hh] for hh in heads]
    no_mask = jnp.zeros((t, MASK_LANES), BF)

    s_own = [_dot_nt(jnp.concatenate([kh_ref[0, hh, qi], no_mask], axis=1), q_aug[hh]) for hh in heads]
    carry = []
    for hh in heads:
        s = jnp.where(causal, s_own[hh], NEG)
        m = jnp.max(s, axis=0, keepdims=True)
        carry.append((m, jnp.exp2(s - m).astype(BF)))
    carry = tuple((m, _dot(vt_ref[0, hh, qi], p)) for hh, (m, p) in zip(heads, carry))

    def past_group(g, carry):
        j0 = g * grp
        s_all = []
        for hh in heads:
            k_aug = jnp.concatenate([kh_ref[0, hh, pl.ds(j0, grp)].reshape(grp * t, LANES),
                                     oh_ref[hh, pl.ds(j0, grp)].reshape(grp * t, MASK_LANES)], axis=1)
            s_all.append(_dot_nt(k_aug, q_aug[hh]))
        stats = []
        for hh in heads:
            m, _ = carry[hh]
            subs = [s_all[hh][i * t:(i + 1) * t] for i in range(grp)]
            cj = [slope[hh] * ((j0 + i - qi) * t).astype(F32) for i in range(grp)]
            m_new = m
            for i in range(grp):
                m_new = jnp.maximum(m_new, jnp.max(subs[i], axis=0, keepdims=True) + cj[i])
            p = jnp.concatenate([jnp.exp2(subs[i] - (m_new - cj[i])).astype(BF) for i in range(grp)], axis=0)
            stats.append((m_new, jnp.exp2(m - m_new), p))
        out = []
        for hh in heads:
            m_new, alpha, p = stats[hh]
            vt = jnp.concatenate([vt_ref[0, hh, j0 + i] for i in range(grp)], axis=1)
            out.append((m_new, alpha * carry[hh][1] + _dot(vt, p)))
        return tuple(out)

    carry = lax.fori_loop(0, (qi + grp - 1) // grp, past_group, carry)
    o_ref[0] = jnp.concatenate(
        [(acc[:HEAD_DIM_A] / acc[HEAD_DIM_A:HEAD_DIM_A + 1]).T for (_, acc) in carry], axis=1).astype(o_ref.dtype)


def _moba_prompt(slopes, qh, mb, kh, vt, onehot):
    b, _, s, _ = qh.shape
    t = MOBA_BLOCK
    nblk = s // t
    hg = MOBA_HEADS
    vrows = vt.shape[3]
    return pl.pallas_call(
        _moba_kernel,
        grid_spec=pltpu.PrefetchScalarGridSpec(
            num_scalar_prefetch=1,
            grid=(b, N_HEADS_A // hg, nblk),
            in_specs=[
                pl.BlockSpec((1, hg, t, LANES), lambda bi, g, qi, sl: (bi, g, qi, 0)),
                pl.BlockSpec((1, t, MASK_LANES), lambda bi, g, qi, sl: (bi, qi, 0)),
                pl.BlockSpec((1, hg, nblk, t, LANES), lambda bi, g, qi, sl: (bi, g, 0, 0, 0)),
                pl.BlockSpec((1, hg, nblk, vrows, t), lambda bi, g, qi, sl: (bi, g, 0, 0, 0)),
                pl.BlockSpec((hg, nblk, t, MASK_LANES), lambda bi, g, qi, sl: (g, 0, 0, 0)),
            ],
            out_specs=pl.BlockSpec((1, t, hg * HEAD_DIM_A), lambda bi, g, qi, sl: (bi, qi, g)),
        ),
        out_shape=jax.ShapeDtypeStruct((b, s, WIDTH_A), BF),
        compiler_params=_params("arbitrary", "arbitrary", "arbitrary"),
        name="moba_prompt",
    )(slopes, qh, mb, kh, vt, onehot)


def _lower_bound(lb_logits, layer):
    e = jnp.exp(lb_logits - jnp.max(lb_logits, axis=0, keepdims=True))
    return jnp.sum(e[:layer + 1], axis=0, keepdims=True) / jnp.sum(e, axis=0, keepdims=True)


def _hgrn_kernel(layer, lbl_ref, gn_ref, tri_ref, q_ref, f_ref, i_ref, o_ref, sfin_ref, st_ref):
    ti = pl.program_id(1)
    t = q_ref.shape[1]
    c = HGRN_CHUNK
    heads = range(N_HEADS_B)
    hk = lambda a, h: a[:, h * KEY_DIM_B:(h + 1) * KEY_DIM_B]

    @pl.when(ti == 0)
    def _():
        st_ref[...] = jnp.zeros(st_ref.shape, F32)

    lb = _lower_bound(lbl_ref[...], layer)
    f = lb + (1.0 - lb) * _sigmoid(f_ref[0])
    k = 1.0 - f
    tri = tri_ref[...]
    cum = sum(_dot(tri, part) for part in _split3(jnp.log(f)))
    tot = jnp.concatenate(
        [jnp.broadcast_to(cum[(n + 1) * c - 1:(n + 1) * c], (c, cum.shape[1])) for n in range(t // c)], axis=0)
    qd = (q_ref[0] * jnp.exp(cum)).astype(BF)
    ki = (k * jnp.exp(-cum)).astype(BF)
    kd = (k * jnp.exp(tot - cum)).astype(BF)
    vb = i_ref[0].astype(BF)
    a = [jnp.where(tri > 0, _dot_nt(hk(qd, h), hk(ki, h)), 0.0).astype(BF) for h in heads]
    o_intra = [_dot(a[h], hk(vb, h)) for h in heads]
    st = [st_ref[h] for h in heads]
    outs = [[] for _ in heads]
    for n in range(t // c):
        r = slice(n * c, (n + 1) * c)
        inter = [_dot_nt(hk(qd, h)[r], st[h].astype(BF)) for h in heads]
        upd = [lax.dot_general(hk(vb, h)[r], hk(kd, h)[r], TN, preferred_element_type=F32) for h in heads]
        decay = jnp.exp(tot[(n + 1) * c - 1:(n + 1) * c])
        for h in heads:
            outs[h].append(o_intra[h][r] + inter[h])
            st[h] = st[h] * hk(decay, h) + upd[h]
    for h in heads:
        st_ref[h] = st[h]
        gn = gn_ref[:, h * VAL_DIM_B:(h + 1) * VAL_DIM_B]
        o_ref[0, :, h * VAL_DIM_B:(h + 1) * VAL_DIM_B] = _rms(jnp.concatenate(outs[h], axis=0), gn).astype(o_ref.dtype)

    @pl.when(ti == pl.num_programs(1) - 1)
    def _():
        for h in heads:
            sfin_ref[0, h] = st[h].T


def _chunk_tri(t):
    r = np.arange(t)
    return jnp.asarray((r[:, None] // HGRN_CHUNK == r[None, :] // HGRN_CHUNK) & (r[None, :] <= r[:, None]), BF)


def _hgrn_prompt(layer, lb_logits, gnorm, qb, fb, ib):
    b, s, _ = qb.shape
    t = 256
    tok = pl.BlockSpec((1, t, WIDTH_B), lambda bi, ti: (bi, ti, 0))
    return pl.pallas_call(
        functools.partial(_hgrn_kernel, layer),
        grid=(b, s // t),
        in_specs=[
            pl.BlockSpec(lb_logits.shape, lambda bi, ti: (0, 0)),
            pl.BlockSpec((1, WIDTH_B), lambda bi, ti: (0, 0)),
            pl.BlockSpec((t, t), lambda bi, ti: (0, 0)),
            tok, tok, tok,
        ],
        out_specs=(
            tok,
            pl.BlockSpec((1, N_HEADS_B, KEY_DIM_B, VAL_DIM_B), lambda bi, ti: (bi, 0, 0, 0)),
        ),
        out_shape=(
            jax.ShapeDtypeStruct((b, s, WIDTH_B), BF),
            jax.ShapeDtypeStruct((b, N_HEADS_B, KEY_DIM_B, VAL_DIM_B), F32),
        ),
        scratch_shapes=[pltpu.VMEM((N_HEADS_B, VAL_DIM_B, KEY_DIM_B), F32)],
        compiler_params=_params("arbitrary", "arbitrary"),
        name="hgrn_prompt",
    )(lb_logits, gnorm, _chunk_tri(t), qb, fb, ib)


def _out_kernel(x_ref, oa_ref, ob_ref, g_ref, w2_ref, wa_ref, wb_ref, wo_ref, gf_ref, y_ref):
    x = x_ref[...]
    hn = _rms(x, g_ref[...]).astype(BF)
    d = x.shape[1]

    def proj(lo, hi):
        return _dot(hn, w2_ref[:, lo:hi])

    ya = _dot((oa_ref[...].astype(F32) * _silu(proj(0, WIDTH_A))).astype(BF), wa_ref[...])
    yb = _dot((ob_ref[...].astype(F32) * _silu(proj(WIDTH_A, WIDTH_A + WIDTH_B))).astype(BF), wb_ref[...])
    g0 = WIDTH_A + WIDTH_B
    mixed = _sigmoid(proj(g0, g0 + d)) * ya + _sigmoid(proj(g0 + d, g0 + 2 * d)) * yb
    out = x + _dot(mixed.astype(BF), wo_ref[...])
    y_ref[...] = _rms(out, gf_ref[...])


def _out_proj(x2, oa, ob, g_in, w2, wa, wb, wo, g_fin, tm):
    n, d = x2.shape
    rows = lambda w: pl.BlockSpec((tm, w), lambda i: (i, 0))
    full = lambda a: pl.BlockSpec(a.shape, lambda i: (0, 0))
    return pl.pallas_call(
        _out_kernel,
        grid=(n // tm,),
        in_specs=[rows(d), rows(WIDTH_A), rows(WIDTH_B), full(g_in), full(w2), full(wa), full(wb), full(wo),
                  full(g_fin)],
        out_specs=rows(d),
        out_shape=jax.ShapeDtypeStruct((n, d), F32),
        compiler_params=_params("arbitrary"),
        name="out_proj",
    )(x2, oa, ob, g_in, w2, wa, wb, wo, g_fin)


def _proj_sample_kernel(x_ref, g_ref, w_ref, r_ref):
    hn = _rms(x_ref[...], g_ref[...]).astype(BF)
    for c in range(w_ref.shape[1] // WIDTH_A):
        cols = slice(c * WIDTH_A, (c + 1) * WIDTH_A)
        r_ref[:, cols] = _dot(hn, w_ref[:, cols])


def _proj_sample(x2, g_in, w1):
    n, _ = x2.shape
    return pl.pallas_call(
        _proj_sample_kernel,
        out_shape=jax.ShapeDtypeStruct((n, w1.shape[1]), F32),
        compiler_params=pltpu.CompilerParams(vmem_limit_bytes=VMEM_LIMIT),
        name="proj_sample",
    )(x2, g_in, w1)


PAGE_CHUNK = 16


def _pagesum_kernel(pt_ref, ck_ref, out_ref, buf, sem):
    ch = buf.shape[1]
    half = ch // 2
    nchunks = pt_ref.shape[0] // ch

    def page_of(c, i):
        return pt_ref[c * ch + (2 * i if i < half else 2 * (i - half) + 1)]

    def copy(c, slot, i):
        return pltpu.make_async_copy(ck_ref.at[page_of(c, i)], buf.at[slot, i], sem.at[slot])

    def start(c, slot):
        for i in range(ch):
            copy(c, slot, i).start()

    c = pl.program_id(0)
    slot = c % 2

    @pl.when(c == 0)
    def _():
        out_ref[...] = jnp.zeros(out_ref.shape, F32)
        start(0, 0)

    @pl.when(c + 1 < nchunks)
    def _():
        start(c + 1, 1 - slot)

    for i in range(ch):
        copy(c, slot, i).wait()
    lane = lax.broadcasted_iota(jnp.int32, (HEAD_DIM_A, LANES), 1)
    grp = (c * half) // LANES
    base = (c * half) % LANES
    for h in range(N_HEADS_A):
        feat = slice(h * HEAD_DIM_A, (h + 1) * HEAD_DIM_A)
        tile = out_ref[grp, feat, :]
        for i in range(half):
            red = jnp.sum(buf[slot, i, h] + buf[slot, half + i, h], axis=1, keepdims=True)
            tile = jnp.where(lane == base + i, red, tile)
        out_ref[grp, feat, :] = tile


def _page_sum(pt_flat, cache_kt):
    _, h, dh, page = cache_kt.shape
    nblocks = pt_flat.shape[0] // (MOBA_BLOCK // page)
    shape = (pl.cdiv(nblocks, LANES), WIDTH_A, LANES)
    return pl.pallas_call(
        _pagesum_kernel,
        grid_spec=pltpu.PrefetchScalarGridSpec(
            num_scalar_prefetch=1,
            grid=(pt_flat.shape[0] // PAGE_CHUNK,),
            in_specs=[pl.BlockSpec(memory_space=pl.ANY)],
            out_specs=pl.BlockSpec(shape, lambda i, pt: (0, 0, 0)),
            scratch_shapes=[pltpu.VMEM((2, PAGE_CHUNK, h, dh, page), F32), pltpu.SemaphoreType.DMA((2,))],
        ),
        out_shape=jax.ShapeDtypeStruct(shape, F32),
        compiler_params=_params("arbitrary"),
        name="page_sum",
    )(pt_flat, cache_kt)


def _sample_gate_kernel(past, slope_ref, r_ref, bs_ref, sel_ref, bias_ref):
    dpg = r_ref.shape[1]
    nblk = LANES // dpg
    page = bias_ref.shape[3]
    ppb = MOBA_BLOCK // page
    hrow = lax.broadcasted_iota(jnp.int32, (N_HEADS_A, WIDTH_A), 0)
    hcol = lax.broadcasted_iota(jnp.int32, (N_HEADS_A, WIDTH_A), 1) // HEAD_DIM_A
    lane = lax.broadcasted_iota(jnp.int32, (N_HEADS_A, LANES), 1)
    tok = lax.broadcasted_iota(jnp.int32, (N_HEADS_A, page), 1)
    kh, kl = _split2(bs_ref[0] / MOBA_BLOCK)
    for j in range(dpg):
        qbd = jnp.where(hrow == hcol, jnp.broadcast_to(r_ref[0, j:j + 1, :WIDTH_A], (N_HEADS_A, WIDTH_A)), 0.0)
        qh, ql = _split2(qbd)
        gate = _dot(qh, kh) + _dot(qh, kl) + _dot(ql, kh)
        blk = lane - j * nblk
        x = jnp.where((blk >= 0) & (blk < nblk), gate, -jnp.inf)
        out = jnp.zeros((N_HEADS_A, LANES), jnp.int32)
        for rank in range(MOBA_TOPK):
            m = jnp.max(x, axis=1, keepdims=True)
            first = jnp.min(jnp.where(x == m, blk, LANES), axis=1, keepdims=True)
            out = jnp.where(lane == rank, first, out)
            x = jnp.where(blk == first, -jnp.inf, x)
            for p in range(ppb):
                dist = past - (first * MOBA_BLOCK + p * page) - tok
                bias_ref[j, rank * ppb + p] = -slope_ref[...] * dist.astype(F32)
        for unused in range(MOBA_TOPK * ppb, bias_ref.shape[1]):
            bias_ref[j, unused] = jnp.zeros((N_HEADS_A, page), F32)
        sel_ref[j] = out


def _sample_gate(past, r, bsum_t, nblk, page):
    ndb = r.shape[0]
    dpg = LANES // nblk
    slots = pl.cdiv(MOBA_TOPK * (MOBA_BLOCK // page), 8) * 8
    slope_col = jnp.asarray(_alibi_slopes(N_HEADS_A), F32)[:, None]
    return pl.pallas_call(
        functools.partial(_sample_gate_kernel, past),
        grid=(ndb // dpg,),
        in_specs=[pl.BlockSpec((N_HEADS_A, 1), lambda i: (0, 0)),
                  pl.BlockSpec((1, dpg, r.shape[1]), lambda i: (i, 0, 0)),
                  pl.BlockSpec((1, WIDTH_A, LANES), lambda i: (i, 0, 0))],
        out_specs=(pl.BlockSpec((dpg, N_HEADS_A, LANES), lambda i: (i, 0, 0)),
                   pl.BlockSpec((dpg, slots, N_HEADS_A, page), lambda i: (i, 0, 0, 0))),
        out_shape=(jax.ShapeDtypeStruct((ndb, N_HEADS_A, LANES), jnp.int32),
                   jax.ShapeDtypeStruct((ndb, slots, N_HEADS_A, page), F32)),
        compiler_params=_params("arbitrary"),
        name="sample_gate",
    )(slope_col, r.reshape(ndb // dpg, dpg, r.shape[1]), bsum_t)


def _sample_attn_kernel(sel_ref, pt_ref, qkv_ref, bias_ref, ck_ref, cv_ref, o_ref, kbuf, vbuf, sem):
    db = pl.program_id(0)
    ndb = pl.num_programs(0)
    npg = pt_ref.shape[0] // ndb
    ntp = kbuf.shape[1]
    ppb = ntp // MOBA_TOPK

    def copies(d, slot):
        out = []
        for h in range(N_HEADS_A):
            for tp in range(ntp):
                blk = sel_ref[(d * N_HEADS_A + h) * MOBA_TOPK + tp // ppb]
                pg = pt_ref[d * npg + blk * ppb + tp % ppb]
                out.append(pltpu.make_async_copy(ck_ref.at[pg, h], kbuf.at[slot, tp, h], sem.at[0, slot]))
                out.append(pltpu.make_async_copy(cv_ref.at[pg, h], vbuf.at[slot, tp, h], sem.at[1, slot]))
        return out

    slot = db % 2

    @pl.when(db == 0)
    def _():
        for cp in copies(0, 0):
            cp.start()

    @pl.when(db + 1 < ndb)
    def _():
        for cp in copies(db + 1, 1 - slot):
            cp.start()

    for cp in copies(db, slot):
        cp.wait()

    hd = (N_HEADS_A, HEAD_DIM_A, 1)
    qkv = qkv_ref[0]
    qc = (qkv[:WIDTH_A] * HEAD_DIM_A ** -0.5).reshape(hd)
    knc = qkv[WIDTH_A:2 * WIDTH_A].reshape(hd)
    vnc = qkv[2 * WIDTH_A:].reshape(hd)
    lg_new = jnp.sum(qc * knc, axis=1)
    lg = jnp.sum(kbuf[slot] * qc[None], axis=2) + bias_ref[0, :ntp]
    m = jnp.maximum(jnp.max(jnp.max(lg, axis=0), axis=1, keepdims=True), lg_new)
    p_new = jnp.exp(lg_new - m)
    p = jnp.exp(lg - m[None])
    l = jnp.sum(jnp.sum(p, axis=0), axis=1, keepdims=True) + p_new
    pv = jnp.sum(vbuf[slot] * p[:, :, None, :], axis=0)
    acc = jnp.sum(pv, axis=2, keepdims=True) + p_new[:, :, None] * vnc
    o_ref[0] = (acc / l[:, :, None]).reshape(WIDTH_A, 1)


def _sample_attn(sel_flat, pt_flat, qkv_col, bias, cache_kt, cache_vt):
    ndb = qkv_col.shape[0]
    _, h, dh, page = cache_kt.shape
    ntp = MOBA_TOPK * (MOBA_BLOCK // page)
    return pl.pallas_call(
        _sample_attn_kernel,
        grid_spec=pltpu.PrefetchScalarGridSpec(
            num_scalar_prefetch=2,
            grid=(ndb,),
            in_specs=[pl.BlockSpec((1, qkv_col.shape[1], 1), lambda d, s, p: (d, 0, 0)),
                      pl.BlockSpec((1,) + bias.shape[1:], lambda d, s, p: (d, 0, 0, 0)),
                      pl.BlockSpec(memory_space=pl.ANY), pl.BlockSpec(memory_space=pl.ANY)],
            out_specs=pl.BlockSpec((1, WIDTH_A, 1), lambda d, s, p: (d, 0, 0)),
            scratch_shapes=[pltpu.VMEM((2, ntp, h, dh, page), F32),
                            pltpu.VMEM((2, ntp, h, dh, page), F32),
                            pltpu.SemaphoreType.DMA((2, 2))],
        ),
        out_shape=jax.ShapeDtypeStruct((ndb, WIDTH_A, 1), F32),
        compiler_params=_params("arbitrary"),
        name="sample_attn",
    )(sel_flat, pt_flat, qkv_col, bias, cache_kt, cache_vt)


def _hgrn_sample_kernel(layer, lbl_ref, gn_ref, r_ref, s_ref, o_ref, sn_ref):
    ndb = r_ref.shape[0]
    kk = KEY_DIM_B
    eye = (lax.broadcasted_iota(jnp.int32, (kk, kk), 0) == lax.broadcasted_iota(jnp.int32, (kk, kk), 1))
    ones = jnp.ones((kk, VAL_DIM_B), BF)

    def column(vec):
        diag = jnp.where(eye, jnp.broadcast_to(vec, (kk, kk)), 0.0)
        return sum(_dot(part, ones) for part in _split3(diag))

    lb_all = _lower_bound(lbl_ref[...], layer)
    q0, f0, i0 = WIDTH_A * 3, WIDTH_A * 3 + WIDTH_B, WIDTH_A * 3 + 2 * WIDTH_B
    for db in range(ndb):
        outs = []
        for h in range(N_HEADS_B):
            cols = slice(h * kk, (h + 1) * kk)
            lb = lb_all[:, cols]
            q = r_ref[db:db + 1, q0 + h * kk:q0 + (h + 1) * kk]
            f = lb + (1.0 - lb) * _sigmoid(r_ref[db:db + 1, f0 + h * kk:f0 + (h + 1) * kk])
            v = r_ref[db:db + 1, i0 + h * VAL_DIM_B:i0 + (h + 1) * VAL_DIM_B]
            fcol = column(f)
            s_new = fcol * s_ref[db, h] + (1.0 - fcol) * v
            sn_ref[db, h] = s_new
            o = jnp.sum(column(q) * s_new, axis=0, keepdims=True)
            outs.append(_rms(o, gn_ref[:, h * VAL_DIM_B:(h + 1) * VAL_DIM_B]))
        o_ref[db:db + 1, :] = jnp.concatenate(outs, axis=1).astype(o_ref.dtype)


def _hgrn_sample(layer, lb_logits, gnorm, r, state):
    ndb = r.shape[0]
    tdb = 8
    full = lambda a: pl.BlockSpec(a.shape, lambda i: (0, 0))
    st = pl.BlockSpec((tdb, N_HEADS_B, KEY_DIM_B, VAL_DIM_B), lambda i: (i, 0, 0, 0))
    return pl.pallas_call(
        functools.partial(_hgrn_sample_kernel, layer),
        grid=(ndb // tdb,),
        in_specs=[full(lb_logits), full(gnorm), pl.BlockSpec((tdb, r.shape[1]), lambda i: (i, 0)), st],
        out_specs=(pl.BlockSpec((tdb, WIDTH_B), lambda i: (i, 0)), st),
        out_shape=(jax.ShapeDtypeStruct((ndb, WIDTH_B), BF), jax.ShapeDtypeStruct(state.shape, F32)),
        compiler_params=_params("arbitrary"),
        name="hgrn_sample",
    )(lb_logits, gnorm, r, state)


def _q_extension():
    ext = np.zeros((N_HEADS_A, LANES), np.float32)
    for h, slope in enumerate(_alibi_slopes(N_HEADS_A)):
        rest = np.float32(slope * LOG2E)
        for part in range(SLOPE_PARTS):
            piece = np.float32(rest.astype(jnp.bfloat16))
            ext[h, SLOPE_LANE + part] = piece
            rest = np.float32(rest - piece)
    return jnp.asarray(ext)


def _block_onehot(nblk, t):
    lane = np.arange(MASK_LANES)[None, None, :]
    tgt = (np.arange(nblk)[None, :, None] * N_HEADS_A + np.arange(N_HEADS_A)[:, None, None])
    oh = (lane == tgt).astype(np.float32)
    return jnp.broadcast_to(jnp.asarray(oh, BF)[:, :, None, :], (N_HEADS_A, nblk, t, MASK_LANES))


def kernel(x_prompt, x_sample, cache_k, cache_v, state_hgrn, page_table, norm_in_g, w_in,
           hgrn_lb_logits, hgrn_norm_g, w_branch_a, w_branch_b, w_out, final_norm_g):
    b, s, d = x_prompt.shape
    ndb, ds, _ = x_sample.shape
    depth, n_pool, _, page, _ = cache_k.shape
    npg = page_table.shape[1]
    past = npg * page
    ppb = MOBA_BLOCK // page
    assert depth == 1 and ds == 1
    assert s % (MOBA_BLOCK * MOBA_GROUP) == 0 and s // MOBA_BLOCK <= MAX_BLOCKS
    assert past % MOBA_BLOCK == 0 and past // MOBA_BLOCK >= MOBA_TOPK
    assert ppb == 2 and npg % PAGE_CHUNK == 0
    nblk_s = npg // ppb
    assert LANES % nblk_s == 0 and ndb % (LANES // nblk_s) == 0
    layer = 0

    wk = N_HEADS_B * KEY_DIM_B
    a3 = 3 * WIDTH_A
    wl = w_in[layer]
    w1 = jnp.concatenate([wl[:, :a3], wl[:, a3 + WIDTH_A:a3 + WIDTH_A + 2 * wk + WIDTH_B]], axis=1).astype(BF)
    zb0 = a3 + WIDTH_A + 2 * wk + WIDTH_B
    w2 = jnp.concatenate([wl[:, a3:a3 + WIDTH_A], wl[:, zb0:]], axis=1).astype(BF)
    wa = w_branch_a[layer].astype(BF)
    wb = w_branch_b[layer].astype(BF)
    wo = w_out[layer].astype(BF)
    g_in = norm_in_g[layer][None, :]
    g_fin = final_norm_g[None, :]
    gnorm = hgrn_norm_g[layer][None, :]
    slopes = jnp.asarray([m * LOG2E for m in _alibi_slopes(N_HEADS_A)], F32)

    qh, kh, mb, vt, k_pg, v_pg, qb, fb, ib = _proj_prompt(x_prompt, g_in, w1, _q_extension(), page)
    o_a = _moba_prompt(slopes, qh, mb, kh, vt, _block_onehot(s // MOBA_BLOCK, MOBA_BLOCK))
    o_b, s_prompt = _hgrn_prompt(layer, hgrn_lb_logits, gnorm, qb, fb, ib)
    y_prompt = _out_proj(x_prompt.reshape(b * s, d), o_a.reshape(b * s, WIDTH_A), o_b.reshape(b * s, WIDTH_B),
                         g_in, w2, wa, wb, wo, g_fin, 512).reshape(b, s, d)

    xs = x_sample.reshape(ndb, d)
    r = _proj_sample(xs, g_in, w1)
    pt_flat = page_table.reshape(ndb * npg)
    ckt = jnp.swapaxes(cache_k[layer], -1, -2)
    cvt = jnp.swapaxes(cache_v[layer], -1, -2)
    bsum_t = _page_sum(pt_flat, ckt)
    sel, bias = _sample_gate(past, r, bsum_t, nblk_s, page)
    sel = sel[:, :, :MOBA_TOPK].reshape(ndb * N_HEADS_A * MOBA_TOPK)
    qkv_col = r[:, :3 * WIDTH_A].reshape(ndb, 3 * WIDTH_A, 1)
    oa_s = _sample_attn(sel, pt_flat, qkv_col, bias, ckt, cvt).reshape(ndb, WIDTH_A)
    ob_s, s_sample = _hgrn_sample(layer, hgrn_lb_logits, gnorm, r, state_hgrn[layer])
    y_sample = _out_proj(xs, oa_s.astype(BF), ob_s, g_in, w2, wa, wb, wo, g_fin, ndb).reshape(ndb, 1, d)

    k_s = r[:, WIDTH_A:2 * WIDTH_A].reshape(ndb, N_HEADS_A, 1, HEAD_DIM_A)
    v_s = r[:, 2 * WIDTH_A:3 * WIDTH_A].reshape(ndb, N_HEADS_A, 1, HEAD_DIM_A)
    k_p = jnp.swapaxes(k_pg, -1, -2)
    v_p = jnp.swapaxes(v_pg, -1, -2)
    return (y_prompt, y_sample, k_p[None], v_p[None], s_prompt[None], k_s[None], v_s[None], s_sample[None])
```

```python
import functools

import numpy as np
import jax
import jax.numpy as jnp
from jax import lax
from jax.experimental import pallas as pl
from jax.experimental.pallas import tpu as pltpu

N_HEADS_A = 8
HEAD_DIM_A = 64
WIDTH_A = N_HEADS_A * HEAD_DIM_A
MOBA_BLOCK = 256
MOBA_TOPK = 3
N_HEADS_B = 4
KEY_DIM_B = 128
VAL_DIM_B = 128
WIDTH_B = N_HEADS_B * VAL_DIM_B
HGRN_CHUNK = 32
NORM_EPS = 1e-6

LANES = 128
MASK_LANES = LANES
MAX_BLOCKS = MASK_LANES // N_HEADS_A
NEG = -(2.0 ** 30)
SLOPE_LANE = HEAD_DIM_A
SLOPE_PARTS = 3
LOG2E = 1.4426950408889634
BF16_ROWS = 16
V_ROWS = HEAD_DIM_A + BF16_ROWS
MOBA_HEADS = 4
MOBA_GROUP = 2
VMEM_LIMIT = 48 * 1024 * 1024

F32 = jnp.float32
BF = jnp.bfloat16
NT = (((1,), (1,)), ((), ()))
TN = (((0,), (0,)), ((), ()))


def _alibi_slopes(n):
    return [2.0 ** (-8.0 * (i + 1) / n) for i in range(n)]


def _rms(x, g):
    ms = jnp.mean(x * x, axis=-1, keepdims=True)
    return x * lax.rsqrt(ms + NORM_EPS) * g


def _split2(a):
    hi = a.astype(BF)
    lo = (a - hi.astype(F32)).astype(BF)
    return hi, lo


def _split3(a):
    hi = a.astype(BF)
    r = a - hi.astype(F32)
    mid = r.astype(BF)
    lo = (r - mid.astype(F32)).astype(BF)
    return hi, mid, lo


def _dot(a, b):
    return jnp.dot(a, b, preferred_element_type=F32)


def _dot_nt(a, b):
    return lax.dot_general(a, b, NT, preferred_element_type=F32)


def _dot3_nt(a, b):
    ah, al = _split2(a)
    bh, bl = _split2(b)
    return _dot_nt(ah, bh) + _dot_nt(ah, bl) + _dot_nt(al, bh)


def _sigmoid(x):
    return 1.0 / (1.0 + jnp.exp(-x))


def _silu(x):
    return x * _sigmoid(x)


def _params(*sem):
    return pltpu.CompilerParams(dimension_semantics=sem, vmem_limit_bytes=VMEM_LIMIT)


def _topk_blocks(gate, valid):
    x = jnp.where(valid, gate, -jnp.inf)
    blk1 = lax.broadcasted_iota(jnp.int32, (1, MASK_LANES), 1) // N_HEADS_A
    rank = jnp.zeros(gate.shape, F32)
    for kb in range(1, MAX_BLOCKS):
        y = pltpu.roll(x, kb * N_HEADS_A, axis=1)
        lower_index = jnp.where(blk1 >= kb, 1.0, 0.0)
        rank = rank + jnp.where(y > x, 1.0, jnp.where(y == x, lower_index, 0.0))
    return jnp.where(valid, jnp.where(rank < MOBA_TOPK, 1.0, 0.0), 0.0)


def _proj_prompt_kernel(x_ref, g_ref, w_ref, qext_ref, qh_ref, kh_ref, mb_ref, vt_ref,
                        kpg_ref, vpg_ref, qb_ref, fb_ref, ib_ref, km_ref):
    i = pl.program_id(1)
    t = x_ref.shape[1]
    page = kpg_ref.shape[4]
    @pl.when(i == 0)
    def _():
        km_ref[...] = jnp.zeros(km_ref.shape, F32)

    hn = _rms(x_ref[0], g_ref[...]).astype(BF)

    def proj(c):
        return _dot(hn, w_ref[:, c * WIDTH_A:(c + 1) * WIDTH_A])

    lane = lax.broadcasted_iota(jnp.int32, (t, LANES), 1)
    row = lax.broadcasted_iota(jnp.int32, (t, LANES), 0)
    q = proj(0)
    gate = _dot3_nt(q, km_ref[...])
    blk = lane // N_HEADS_A
    sel = _topk_blocks(gate, blk < i)
    mb_ref[0] = jnp.where(sel > 0.0, 0.0, NEG).astype(BF)

    k = proj(1)
    kmean = jnp.sum(k, axis=0, keepdims=True) / t
    hrow = lax.broadcasted_iota(jnp.int32, (N_HEADS_A, WIDTH_A), 0)
    hcol = lax.broadcasted_iota(jnp.int32, (N_HEADS_A, WIDTH_A), 1) // HEAD_DIM_A
    km_ref[pl.ds(pl.multiple_of(i * N_HEADS_A, N_HEADS_A), N_HEADS_A), :] = jnp.where(
        hrow == hcol, jnp.broadcast_to(kmean, (N_HEADS_A, WIDTH_A)), 0.0)

    low = lane < HEAD_DIM_A
    kext = jnp.where((lane >= SLOPE_LANE) & (lane < SLOPE_LANE + SLOPE_PARTS), row.astype(F32), 0.0)
    scale = HEAD_DIM_A ** -0.5 * LOG2E
    for hp in range(N_HEADS_A // 2):
        cols = slice(hp * LANES, (hp + 1) * LANES)
        qp = q[:, cols] * scale
        kp = k[:, cols]
        qpr = pltpu.roll(qp, HEAD_DIM_A, axis=1)
        kpr = pltpu.roll(kp, HEAD_DIM_A, axis=1)
        qh_ref[0, 2 * hp] = jnp.where(low, qp, qext_ref[2 * hp:2 * hp + 1, :]).astype(BF)
        qh_ref[0, 2 * hp + 1] = jnp.where(low, qpr, qext_ref[2 * hp + 1:2 * hp + 2, :]).astype(BF)
        kh_ref[0, 2 * hp, 0] = jnp.where(low, kp, kext).astype(BF)
        kh_ref[0, 2 * hp + 1, 0] = jnp.where(low, kpr, kext).astype(BF)

    v = proj(2)
    kt = k.T
    vt = v.T
    ones_row = (lax.broadcasted_iota(jnp.int32, (BF16_ROWS, t), 0) == 0).astype(F32)
    for h in range(N_HEADS_A):
        feat = slice(h * HEAD_DIM_A, (h + 1) * HEAD_DIM_A)
        vt_ref[0, h, 0] = jnp.concatenate([vt[feat, :], ones_row], axis=0).astype(BF)
        for p in range(t // page):
            toks = slice(p * page, (p + 1) * page)
            kpg_ref[0, p, h] = kt[feat, toks]
            vpg_ref[0, p, h] = vt[feat, toks]

    qb_ref[0] = proj(3)
    fb_ref[0] = proj(4)
    ib_ref[0] = proj(5)


def _proj_prompt(x, g_in, w1, qext, page):
    b, s, d = x.shape
    t = MOBA_BLOCK
    nblk = s // t
    nproj = w1.shape[1]
    out_shape = (
        jax.ShapeDtypeStruct((b, N_HEADS_A, s, LANES), BF),
        jax.ShapeDtypeStruct((b, N_HEADS_A, nblk, t, LANES), BF),
        jax.ShapeDtypeStruct((b, s, MASK_LANES), BF),
        jax.ShapeDtypeStruct((b, N_HEADS_A, nblk, V_ROWS, t), BF),
        jax.ShapeDtypeStruct((b, s // page, N_HEADS_A, HEAD_DIM_A, page), F32),
        jax.ShapeDtypeStruct((b, s // page, N_HEADS_A, HEAD_DIM_A, page), F32),
        jax.ShapeDtypeStruct((b, s, WIDTH_B), F32),
        jax.ShapeDtypeStruct((b, s, WIDTH_B), F32),
        jax.ShapeDtypeStruct((b, s, WIDTH_B), F32),
    )
    pg = pl.BlockSpec((1, t // page, N_HEADS_A, HEAD_DIM_A, page), lambda bi, i: (bi, i, 0, 0, 0))
    row3 = lambda w: pl.BlockSpec((1, t, w), lambda bi, i: (bi, i, 0))
    return pl.pallas_call(
        _proj_prompt_kernel,
        grid=(b, nblk),
        in_specs=[
            row3(d),
            pl.BlockSpec((1, d), lambda bi, i: (0, 0)),
            pl.BlockSpec((d, nproj), lambda bi, i: (0, 0)),
            pl.BlockSpec((N_HEADS_A, LANES), lambda bi, i: (0, 0)),
        ],
        out_specs=(
            pl.BlockSpec((1, N_HEADS_A, t, LANES), lambda bi, i: (bi, 0, i, 0)),
            pl.BlockSpec((1, N_HEADS_A, 1, t, LANES), lambda bi, i: (bi, 0, i, 0, 0)),
            row3(MASK_LANES),
            pl.BlockSpec((1, N_HEADS_A, 1, V_ROWS, t), lambda bi, i: (bi, 0, i, 0, 0)),
            pg, pg, row3(WIDTH_B), row3(WIDTH_B), row3(WIDTH_B),
        ),
        out_shape=out_shape,
        scratch_shapes=[pltpu.VMEM((MASK_LANES, WIDTH_A), F32)],
        compiler_params=_params("arbitrary", "arbitrary"),
        name="proj_prompt",
    )(x, g_in, w1, qext)


def _moba_kernel(slopes_ref, pt_ref, qh_ref, mb_ref, kh_ref, vt_ref, oh_ref, ck_ref, o_ref, bs_ref, pbuf, psem):
    hg = pl.program_id(1)
    qi = pl.program_id(2)
    step = (pl.program_id(0) * pl.num_programs(1) + hg) * pl.num_programs(2) + qi
    nsteps = pl.num_programs(0) * pl.num_programs(1) * pl.num_programs(2)
    _page_sums(step, nsteps, pt_ref, ck_ref, bs_ref, pbuf, psem)

    t = mb_ref.shape[1]
    grp = MOBA_GROUP
    mb = mb_ref[0]
    krow = lax.broadcasted_iota(jnp.int32, (t, t), 0)
    qcol = lax.broadcasted_iota(jnp.int32, (t, t), 1)
    causal = krow <= qcol
    heads = range(qh_ref.shape[1])
    q_aug = [jnp.concatenate([qh_ref[0, hh], mb], axis=1) for hh in heads]
    slope = [slopes_ref[hg * len(heads) + hh] for hh in heads]
    no_mask = jnp.zeros((t, MASK_LANES), BF)

    s_own = [_dot_nt(jnp.concatenate([kh_ref[0, hh, qi], no_mask], axis=1), q_aug[hh]) for hh in heads]
    carry = []
    for hh in heads:
        s = jnp.where(causal, s_own[hh], NEG)
        m = jnp.max(s, axis=0, keepdims=True)
        carry.append((m, jnp.exp2(s - m).astype(BF)))
    carry = tuple((m, _dot(vt_ref[0, hh, qi], p)) for hh, (m, p) in zip(heads, carry))

    def past_group(g, carry):
        j0 = g * grp
        s_all = []
        for hh in heads:
            k_aug = jnp.concatenate([kh_ref[0, hh, pl.ds(j0, grp)].reshape(grp * t, LANES),
                                     oh_ref[hh, pl.ds(j0, grp)].reshape(grp * t, MASK_LANES)], axis=1)
            s_all.append(_dot_nt(k_aug, q_aug[hh]))
        stats = []
        for hh in heads:
            m, _ = carry[hh]
            subs = [s_all[hh][i * t:(i + 1) * t] for i in range(grp)]
            cj = [slope[hh] * ((j0 + i - qi) * t).astype(F32) for i in range(grp)]
            m_new = m
            for i in range(grp):
                m_new = jnp.maximum(m_new, jnp.max(subs[i], axis=0, keepdims=True) + cj[i])
            p = jnp.concatenate([jnp.exp2(subs[i] - (m_new - cj[i])).astype(BF) for i in range(grp)], axis=0)
            stats.append((m_new, jnp.exp2(m - m_new), p))
        out = []
        for hh in heads:
            m_new, alpha, p = stats[hh]
            vt = jnp.concatenate([vt_ref[0, hh, j0 + i] for i in range(grp)], axis=1)
            out.append((m_new, alpha * carry[hh][1] + _dot(vt, p)))
        return tuple(out)

    carry = lax.fori_loop(0, (qi + grp - 1) // grp, past_group, carry)
    o_ref[0] = jnp.concatenate(
        [(acc[:HEAD_DIM_A] / acc[HEAD_DIM_A:HEAD_DIM_A + 1]).T for (_, acc) in carry], axis=1).astype(o_ref.dtype)


def _page_sums(step, nsteps, pt_ref, ck_ref, bs_ref, pbuf, psem):
    pps = pbuf.shape[1]
    half = pps // 2

    def copy(s, slot, i):
        page = pt_ref[s * pps + (2 * i if i < half else 2 * (i - half) + 1)]
        return pltpu.make_async_copy(ck_ref.at[page], pbuf.at[slot, i], psem.at[slot])

    slot = step % 2

    @pl.when(step == 0)
    def _():
        for i in range(pps):
            copy(0, 0, i).start()

    @pl.when(step + 1 < nsteps)
    def _():
        for i in range(pps):
            copy(step + 1, 1 - slot, i).start()

    base = (step * half) % LANES

    @pl.when(base == 0)
    def _():
        bs_ref[...] = jnp.zeros(bs_ref.shape, F32)

    for i in range(pps):
        copy(step, slot, i).wait()
    lane = lax.broadcasted_iota(jnp.int32, (HEAD_DIM_A, LANES), 1)
    for h in range(N_HEADS_A):
        feat = slice(h * HEAD_DIM_A, (h + 1) * HEAD_DIM_A)
        tile = bs_ref[0, feat, :]
        for i in range(half):
            red = jnp.sum(pbuf[slot, i, h] + pbuf[slot, half + i, h], axis=1, keepdims=True)
            tile = jnp.where(lane == base + i, red, tile)
        bs_ref[0, feat, :] = tile


def _moba_prompt(slopes, pt_flat, qh, mb, kh, vt, onehot, cache_kt):
    b, _, s, _ = qh.shape
    t = MOBA_BLOCK
    nblk = s // t
    hg = MOBA_HEADS
    vrows = vt.shape[3]
    _, h, dh, page = cache_kt.shape
    ngrp = N_HEADS_A // hg
    nsteps = b * ngrp * nblk
    pps = pt_flat.shape[0] // nsteps
    half = pps // 2
    assert pps * nsteps == pt_flat.shape[0] and pps % 2 == 0 and LANES % half == 0
    bs_shape = (pl.cdiv(nsteps * half, LANES), WIDTH_A, LANES)
    return pl.pallas_call(
        _moba_kernel,
        grid_spec=pltpu.PrefetchScalarGridSpec(
            num_scalar_prefetch=2,
            grid=(b, ngrp, nblk),
            in_specs=[
                pl.BlockSpec((1, hg, t, LANES), lambda bi, g, qi, sl, pt: (bi, g, qi, 0)),
                pl.BlockSpec((1, t, MASK_LANES), lambda bi, g, qi, sl, pt: (bi, qi, 0)),
                pl.BlockSpec((1, hg, nblk, t, LANES), lambda bi, g, qi, sl, pt: (bi, g, 0, 0, 0)),
                pl.BlockSpec((1, hg, nblk, vrows, t), lambda bi, g, qi, sl, pt: (bi, g, 0, 0, 0)),
                pl.BlockSpec((hg, nblk, t, MASK_LANES), lambda bi, g, qi, sl, pt: (g, 0, 0, 0)),
                pl.BlockSpec(memory_space=pl.ANY),
            ],
            out_specs=(
                pl.BlockSpec((1, t, hg * HEAD_DIM_A), lambda bi, g, qi, sl, pt: (bi, qi, g)),
                pl.BlockSpec((1, WIDTH_A, LANES),
                             lambda bi, g, qi, sl, pt: ((((bi * ngrp + g) * nblk + qi) * half) // LANES, 0, 0)),
            ),
            scratch_shapes=[pltpu.VMEM((2, pps, h, dh, page), F32), pltpu.SemaphoreType.DMA((2,))],
        ),
        out_shape=(jax.ShapeDtypeStruct((b, s, WIDTH_A), BF), jax.ShapeDtypeStruct(bs_shape, F32)),
        compiler_params=_params("arbitrary", "arbitrary", "arbitrary"),
        name="moba_prompt",
    )(slopes, pt_flat, qh, mb, kh, vt, onehot, cache_kt)


def _lower_bound(lb_logits, layer):
    e = jnp.exp(lb_logits - jnp.max(lb_logits, axis=0, keepdims=True))
    return jnp.sum(e[:layer + 1], axis=0, keepdims=True) / jnp.sum(e, axis=0, keepdims=True)


def _hgrn_kernel(layer, lbl_ref, gn_ref, tri_ref, q_ref, f_ref, i_ref, o_ref, sfin_ref, st_ref):
    ti = pl.program_id(1)
    t = q_ref.shape[1]
    c = HGRN_CHUNK
    heads = range(N_HEADS_B)
    hk = lambda a, h: a[:, h * KEY_DIM_B:(h + 1) * KEY_DIM_B]

    @pl.when(ti == 0)
    def _():
        st_ref[...] = jnp.zeros(st_ref.shape, F32)

    lb = _lower_bound(lbl_ref[...], layer)
    f = lb + (1.0 - lb) * _sigmoid(f_ref[0])
    k = 1.0 - f
    tri = tri_ref[...]
    cum = sum(_dot(tri, part) for part in _split3(jnp.log(f)))
    tot = jnp.concatenate(
        [jnp.broadcast_to(cum[(n + 1) * c - 1:(n + 1) * c], (c, cum.shape[1])) for n in range(t // c)], axis=0)
    qd = (q_ref[0] * jnp.exp(cum)).astype(BF)
    ki = (k * jnp.exp(-cum)).astype(BF)
    kd = (k * jnp.exp(tot - cum)).astype(BF)
    vb = i_ref[0].astype(BF)
    a = [jnp.where(tri > 0, _dot_nt(hk(qd, h), hk(ki, h)), 0.0).astype(BF) for h in heads]
    o_intra = [_dot(a[h], hk(vb, h)) for h in heads]
    chunks = [slice(n * c, (n + 1) * c) for n in range(t // c)]
    upd = [[lax.dot_general(hk(vb, h)[r], hk(kd, h)[r], TN, preferred_element_type=F32) for h in heads]
           for r in chunks]
    st = [st_ref[h] for h in heads]
    st_in = []
    for n, r in enumerate(chunks):
        st_in.append([st[h].astype(BF) for h in heads])
        decay = jnp.exp(tot[r.stop - 1:r.stop])
        st = [st[h] * hk(decay, h) + upd[n][h] for h in heads]
    inter = [[_dot_nt(hk(qd, h)[r], st_in[n][h]) for h in heads] for n, r in enumerate(chunks)]
    for h in heads:
        st_ref[h] = st[h]
        gn = gn_ref[:, h * VAL_DIM_B:(h + 1) * VAL_DIM_B]
        o = o_intra[h] + jnp.concatenate([inter[n][h] for n in range(len(chunks))], axis=0)
        o_ref[0, :, h * VAL_DIM_B:(h + 1) * VAL_DIM_B] = _rms(o, gn).astype(o_ref.dtype)

    @pl.when(ti == pl.num_programs(1) - 1)
    def _():
        for h in heads:
            sfin_ref[0, h] = st[h].T


def _chunk_tri(t):
    r = np.arange(t)
    return jnp.asarray((r[:, None] // HGRN_CHUNK == r[None, :] // HGRN_CHUNK) & (r[None, :] <= r[:, None]), BF)


def _hgrn_prompt(layer, lb_logits, gnorm, qb, fb, ib):
    b, s, _ = qb.shape
    t = 256
    tok = pl.BlockSpec((1, t, WIDTH_B), lambda bi, ti: (bi, ti, 0))
    return pl.pallas_call(
        functools.partial(_hgrn_kernel, layer),
        grid=(b, s // t),
        in_specs=[
            pl.BlockSpec(lb_logits.shape, lambda bi, ti: (0, 0)),
            pl.BlockSpec((1, WIDTH_B), lambda bi, ti: (0, 0)),
            pl.BlockSpec((t, t), lambda bi, ti: (0, 0)),
            tok, tok, tok,
        ],
        out_specs=(
            tok,
            pl.BlockSpec((1, N_HEADS_B, KEY_DIM_B, VAL_DIM_B), lambda bi, ti: (bi, 0, 0, 0)),
        ),
        out_shape=(
            jax.ShapeDtypeStruct((b, s, WIDTH_B), BF),
            jax.ShapeDtypeStruct((b, N_HEADS_B, KEY_DIM_B, VAL_DIM_B), F32),
        ),
        scratch_shapes=[pltpu.VMEM((N_HEADS_B, VAL_DIM_B, KEY_DIM_B), F32)],
        compiler_params=_params("arbitrary", "arbitrary"),
        name="hgrn_prompt",
    )(lb_logits, gnorm, _chunk_tri(t), qb, fb, ib)


def _out_kernel(x_ref, oa_ref, ob_ref, g_ref, w2_ref, wa_ref, wb_ref, wo_ref, gf_ref, y_ref):
    x = x_ref[...]
    hn = _rms(x, g_ref[...]).astype(BF)
    d = x.shape[1]

    def proj(lo, hi):
        return _dot(hn, w2_ref[:, lo:hi])

    ya = _dot((oa_ref[...].astype(F32) * _silu(proj(0, WIDTH_A))).astype(BF), wa_ref[...])
    yb = _dot((ob_ref[...].astype(F32) * _silu(proj(WIDTH_A, WIDTH_A + WIDTH_B))).astype(BF), wb_ref[...])
    g0 = WIDTH_A + WIDTH_B
    mixed = _sigmoid(proj(g0, g0 + d)) * ya + _sigmoid(proj(g0 + d, g0 + 2 * d)) * yb
    out = x + _dot(mixed.astype(BF), wo_ref[...])
    y_ref[...] = _rms(out, gf_ref[...])


def _out_proj(x2, oa, ob, g_in, w2, wa, wb, wo, g_fin, tm):
    n, d = x2.shape
    rows = lambda w: pl.BlockSpec((tm, w), lambda i: (i, 0))
    full = lambda a: pl.BlockSpec(a.shape, lambda i: (0, 0))
    return pl.pallas_call(
        _out_kernel,
        grid=(n // tm,),
        in_specs=[rows(d), rows(WIDTH_A), rows(WIDTH_B), full(g_in), full(w2), full(wa), full(wb), full(wo),
                  full(g_fin)],
        out_specs=rows(d),
        out_shape=jax.ShapeDtypeStruct((n, d), F32),
        compiler_params=_params("arbitrary"),
        name="out_proj",
    )(x2, oa, ob, g_in, w2, wa, wb, wo, g_fin)


def _proj_sample_kernel(x_ref, g_ref, w_ref, r_ref):
    hn = _rms(x_ref[...], g_ref[...]).astype(BF)
    for c in range(w_ref.shape[1] // WIDTH_A):
        cols = slice(c * WIDTH_A, (c + 1) * WIDTH_A)
        r_ref[:, cols] = _dot(hn, w_ref[:, cols])


def _proj_sample(x2, g_in, w1):
    n, _ = x2.shape
    return pl.pallas_call(
        _proj_sample_kernel,
        out_shape=jax.ShapeDtypeStruct((n, w1.shape[1]), F32),
        compiler_params=pltpu.CompilerParams(vmem_limit_bytes=VMEM_LIMIT),
        name="proj_sample",
    )(x2, g_in, w1)


def _sample_gate_kernel(past, slope_ref, r_ref, bs_ref, sel_ref, bias_ref):
    dpg = r_ref.shape[1]
    nblk = LANES // dpg
    page = bias_ref.shape[3]
    ppb = MOBA_BLOCK // page
    hrow = lax.broadcasted_iota(jnp.int32, (N_HEADS_A, WIDTH_A), 0)
    hcol = lax.broadcasted_iota(jnp.int32, (N_HEADS_A, WIDTH_A), 1) // HEAD_DIM_A
    lane = lax.broadcasted_iota(jnp.int32, (N_HEADS_A, LANES), 1)
    tok = lax.broadcasted_iota(jnp.int32, (N_HEADS_A, page), 1)
    kh, kl = _split2(bs_ref[0] / MOBA_BLOCK)
    for j in range(dpg):
        qbd = jnp.where(hrow == hcol, jnp.broadcast_to(r_ref[0, j:j + 1, :WIDTH_A], (N_HEADS_A, WIDTH_A)), 0.0)
        qh, ql = _split2(qbd)
        gate = _dot(qh, kh) + _dot(qh, kl) + _dot(ql, kh)
        blk = lane - j * nblk
        x = jnp.where((blk >= 0) & (blk < nblk), gate, -jnp.inf)
        out = jnp.zeros((N_HEADS_A, LANES), jnp.int32)
        for rank in range(MOBA_TOPK):
            m = jnp.max(x, axis=1, keepdims=True)
            first = jnp.min(jnp.where(x == m, blk, LANES), axis=1, keepdims=True)
            out = jnp.where(lane == rank, first, out)
            x = jnp.where(blk == first, -jnp.inf, x)
            for p in range(ppb):
                dist = past - (first * MOBA_BLOCK + p * page) - tok
                bias_ref[j, rank * ppb + p] = -slope_ref[...] * dist.astype(F32)
        for unused in range(MOBA_TOPK * ppb, bias_ref.shape[1]):
            bias_ref[j, unused] = jnp.zeros((N_HEADS_A, page), F32)
        sel_ref[j] = out


def _sample_gate(past, r, bsum_t, nblk, page):
    ndb = r.shape[0]
    dpg = LANES // nblk
    slots = pl.cdiv(MOBA_TOPK * (MOBA_BLOCK // page), 8) * 8
    slope_col = jnp.asarray(_alibi_slopes(N_HEADS_A), F32)[:, None]
    return pl.pallas_call(
        functools.partial(_sample_gate_kernel, past),
        grid=(ndb // dpg,),
        in_specs=[pl.BlockSpec((N_HEADS_A, 1), lambda i: (0, 0)),
                  pl.BlockSpec((1, dpg, r.shape[1]), lambda i: (i, 0, 0)),
                  pl.BlockSpec((1, WIDTH_A, LANES), lambda i: (i, 0, 0))],
        out_specs=(pl.BlockSpec((dpg, N_HEADS_A, LANES), lambda i: (i, 0, 0)),
                   pl.BlockSpec((dpg, slots, N_HEADS_A, page), lambda i: (i, 0, 0, 0))),
        out_shape=(jax.ShapeDtypeStruct((ndb, N_HEADS_A, LANES), jnp.int32),
                   jax.ShapeDtypeStruct((ndb, slots, N_HEADS_A, page), F32)),
        compiler_params=_params("arbitrary"),
        name="sample_gate",
    )(slope_col, r.reshape(ndb // dpg, dpg, r.shape[1]), bsum_t)


def _sample_attn_kernel(sel_ref, pt_ref, qkv_ref, bias_ref, ck_ref, cv_ref, o_ref, kbuf, vbuf, sem):
    db = pl.program_id(0)
    ndb = pl.num_programs(0)
    npg = pt_ref.shape[0] // ndb
    ntp = kbuf.shape[1]
    ppb = ntp // MOBA_TOPK

    def copies(d, slot):
        out = []
        for h in range(N_HEADS_A):
            for tp in range(ntp):
                blk = sel_ref[(d * N_HEADS_A + h) * MOBA_TOPK + tp // ppb]
                pg = pt_ref[d * npg + blk * ppb + tp % ppb]
                out.append(pltpu.make_async_copy(ck_ref.at[pg, h], kbuf.at[slot, tp, h], sem.at[0, slot]))
                out.append(pltpu.make_async_copy(cv_ref.at[pg, h], vbuf.at[slot, tp, h], sem.at[1, slot]))
        return out

    slot = db % 2

    @pl.when(db == 0)
    def _():
        for cp in copies(0, 0):
            cp.start()

    @pl.when(db + 1 < ndb)
    def _():
        for cp in copies(db + 1, 1 - slot):
            cp.start()

    for cp in copies(db, slot):
        cp.wait()

    hd = (N_HEADS_A, HEAD_DIM_A, 1)
    qkv = qkv_ref[0]
    qc = (qkv[:WIDTH_A] * HEAD_DIM_A ** -0.5).reshape(hd)
    knc = qkv[WIDTH_A:2 * WIDTH_A].reshape(hd)
    vnc = qkv[2 * WIDTH_A:].reshape(hd)
    lg_new = jnp.sum(qc * knc, axis=1)
    lg = jnp.sum(kbuf[slot] * qc[None], axis=2) + bias_ref[0, :ntp]
    m = jnp.maximum(jnp.max(jnp.max(lg, axis=0), axis=1, keepdims=True), lg_new)
    p_new = jnp.exp(lg_new - m)
    p = jnp.exp(lg - m[None])
    l = jnp.sum(jnp.sum(p, axis=0), axis=1, keepdims=True) + p_new
    pv = jnp.sum(vbuf[slot] * p[:, :, None, :], axis=0)
    acc = jnp.sum(pv, axis=2, keepdims=True) + p_new[:, :, None] * vnc
    o_ref[0] = (acc / l[:, :, None]).reshape(WIDTH_A, 1)


def _sample_attn(sel_flat, pt_flat, qkv_col, bias, cache_kt, cache_vt):
    ndb = qkv_col.shape[0]
    _, h, dh, page = cache_kt.shape
    ntp = MOBA_TOPK * (MOBA_BLOCK // page)
    return pl.pallas_call(
        _sample_attn_kernel,
        grid_spec=pltpu.PrefetchScalarGridSpec(
            num_scalar_prefetch=2,
            grid=(ndb,),
            in_specs=[pl.BlockSpec((1, qkv_col.shape[1], 1), lambda d, s, p: (d, 0, 0)),
                      pl.BlockSpec((1,) + bias.shape[1:], lambda d, s, p: (d, 0, 0, 0)),
                      pl.BlockSpec(memory_space=pl.ANY), pl.BlockSpec(memory_space=pl.ANY)],
            out_specs=pl.BlockSpec((1, WIDTH_A, 1), lambda d, s, p: (d, 0, 0)),
            scratch_shapes=[pltpu.VMEM((2, ntp, h, dh, page), F32),
                            pltpu.VMEM((2, ntp, h, dh, page), F32),
                            pltpu.SemaphoreType.DMA((2, 2))],
        ),
        out_shape=jax.ShapeDtypeStruct((ndb, WIDTH_A, 1), F32),
        compiler_params=_params("arbitrary"),
        name="sample_attn",
    )(sel_flat, pt_flat, qkv_col, bias, cache_kt, cache_vt)


def _hgrn_sample_kernel(layer, lbl_ref, gn_ref, r_ref, s_ref, o_ref, sn_ref):
    ndb = r_ref.shape[0]
    kk = KEY_DIM_B
    eye = (lax.broadcasted_iota(jnp.int32, (kk, kk), 0) == lax.broadcasted_iota(jnp.int32, (kk, kk), 1))
    ones = jnp.ones((kk, VAL_DIM_B), BF)

    def column(vec):
        diag = jnp.where(eye, jnp.broadcast_to(vec, (kk, kk)), 0.0)
        return sum(_dot(part, ones) for part in _split3(diag))

    lb_all = _lower_bound(lbl_ref[...], layer)
    q0, f0, i0 = WIDTH_A * 3, WIDTH_A * 3 + WIDTH_B, WIDTH_A * 3 + 2 * WIDTH_B
    for db in range(ndb):
        outs = []
        for h in range(N_HEADS_B):
            cols = slice(h * kk, (h + 1) * kk)
            lb = lb_all[:, cols]
            q = r_ref[db:db + 1, q0 + h * kk:q0 + (h + 1) * kk]
            f = lb + (1.0 - lb) * _sigmoid(r_ref[db:db + 1, f0 + h * kk:f0 + (h + 1) * kk])
            v = r_ref[db:db + 1, i0 + h * VAL_DIM_B:i0 + (h + 1) * VAL_DIM_B]
            fcol = column(f)
            s_new = fcol * s_ref[db, h] + (1.0 - fcol) * v
            sn_ref[db, h] = s_new
            o = jnp.sum(column(q) * s_new, axis=0, keepdims=True)
            outs.append(_rms(o, gn_ref[:, h * VAL_DIM_B:(h + 1) * VAL_DIM_B]))
        o_ref[db:db + 1, :] = jnp.concatenate(outs, axis=1).astype(o_ref.dtype)


def _hgrn_sample(layer, lb_logits, gnorm, r, state):
    ndb = r.shape[0]
    tdb = 8
    full = lambda a: pl.BlockSpec(a.shape, lambda i: (0, 0))
    st = pl.BlockSpec((tdb, N_HEADS_B, KEY_DIM_B, VAL_DIM_B), lambda i: (i, 0, 0, 0))
    return pl.pallas_call(
        functools.partial(_hgrn_sample_kernel, layer),
        grid=(ndb // tdb,),
        in_specs=[full(lb_logits), full(gnorm), pl.BlockSpec((tdb, r.shape[1]), lambda i: (i, 0)), st],
        out_specs=(pl.BlockSpec((tdb, WIDTH_B), lambda i: (i, 0)), st),
        out_shape=(jax.ShapeDtypeStruct((ndb, WIDTH_B), BF), jax.ShapeDtypeStruct(state.shape, F32)),
        compiler_params=_params("arbitrary"),
        name="hgrn_sample",
    )(lb_logits, gnorm, r, state)


def _q_extension():
    ext = np.zeros((N_HEADS_A, LANES), np.float32)
    for h, slope in enumerate(_alibi_slopes(N_HEADS_A)):
        rest = np.float32(slope * LOG2E)
        for part in range(SLOPE_PARTS):
            piece = np.float32(rest.astype(jnp.bfloat16))
            ext[h, SLOPE_LANE + part] = piece
            rest = np.float32(rest - piece)
    return jnp.asarray(ext)


def _block_onehot(nblk, t):
    lane = np.arange(MASK_LANES)[None, None, :]
    tgt = (np.arange(nblk)[None, :, None] * N_HEADS_A + np.arange(N_HEADS_A)[:, None, None])
    oh = (lane == tgt).astype(np.float32)
    return jnp.broadcast_to(jnp.asarray(oh, BF)[:, :, None, :], (N_HEADS_A, nblk, t, MASK_LANES))


def kernel(x_prompt, x_sample, cache_k, cache_v, state_hgrn, page_table, norm_in_g, w_in,
           hgrn_lb_logits, hgrn_norm_g, w_branch_a, w_branch_b, w_out, final_norm_g):
    b, s, d = x_prompt.shape
    ndb, ds, _ = x_sample.shape
    depth, n_pool, _, page, _ = cache_k.shape
    npg = page_table.shape[1]
    past = npg * page
    ppb = MOBA_BLOCK // page
    assert depth == 1 and ds == 1
    assert s % (MOBA_BLOCK * MOBA_GROUP) == 0 and s // MOBA_BLOCK <= MAX_BLOCKS
    assert past % MOBA_BLOCK == 0 and past // MOBA_BLOCK >= MOBA_TOPK
    assert ppb == 2
    nblk_s = npg // ppb
    assert LANES % nblk_s == 0 and ndb % (LANES // nblk_s) == 0
    layer = 0

    wk = N_HEADS_B * KEY_DIM_B
    a3 = 3 * WIDTH_A
    wl = w_in[layer]
    w1 = jnp.concatenate([wl[:, :a3], wl[:, a3 + WIDTH_A:a3 + WIDTH_A + 2 * wk + WIDTH_B]], axis=1).astype(BF)
    zb0 = a3 + WIDTH_A + 2 * wk + WIDTH_B
    w2 = jnp.concatenate([wl[:, a3:a3 + WIDTH_A], wl[:, zb0:]], axis=1).astype(BF)
    wa = w_branch_a[layer].astype(BF)
    wb = w_branch_b[layer].astype(BF)
    wo = w_out[layer].astype(BF)
    g_in = norm_in_g[layer][None, :]
    g_fin = final_norm_g[None, :]
    gnorm = hgrn_norm_g[layer][None, :]
    slopes = jnp.asarray([m * LOG2E for m in _alibi_slopes(N_HEADS_A)], F32)

    pt_flat = page_table.reshape(ndb * npg)
    ckt = jnp.swapaxes(cache_k[layer], -1, -2)
    cvt = jnp.swapaxes(cache_v[layer], -1, -2)

    qh, kh, mb, vt, k_pg, v_pg, qb, fb, ib = _proj_prompt(x_prompt, g_in, w1, _q_extension(), page)
    o_a, bsum_t = _moba_prompt(slopes, pt_flat, qh, mb, kh, vt, _block_onehot(s // MOBA_BLOCK, MOBA_BLOCK), ckt)
    o_b, s_prompt = _hgrn_prompt(layer, hgrn_lb_logits, gnorm, qb, fb, ib)
    y_prompt = _out_proj(x_prompt.reshape(b * s, d), o_a.reshape(b * s, WIDTH_A), o_b.reshape(b * s, WIDTH_B),
                         g_in, w2, wa, wb, wo, g_fin, 512).reshape(b, s, d)

    xs = x_sample.reshape(ndb, d)
    r = _proj_sample(xs, g_in, w1)
    sel, bias = _sample_gate(past, r, bsum_t, nblk_s, page)
    sel = sel[:, :, :MOBA_TOPK].reshape(ndb * N_HEADS_A * MOBA_TOPK)
    qkv_col = r[:, :3 * WIDTH_A].reshape(ndb, 3 * WIDTH_A, 1)
    oa_s = _sample_attn(sel, pt_flat, qkv_col, bias, ckt, cvt).reshape(ndb, WIDTH_A)
    ob_s, s_sample = _hgrn_sample(layer, hgrn_lb_logits, gnorm, r, state_hgrn[layer])
    y_sample = _out_proj(xs, oa_s.astype(BF), ob_s, g_in, w2, wa, wb, wo, g_fin, ndb).reshape(ndb, 1, d)

    k_s = r[:, WIDTH_A:2 * WIDTH_A].reshape(ndb, N_HEADS_A, 1, HEAD_DIM_A)
    v_s = r[:, 2 * WIDTH_A:3 * WIDTH_A].reshape(ndb, N_HEADS_A, 1, HEAD_DIM_A)
    k_p = jnp.swapaxes(k_pg, -1, -2)
    v_p = jnp.swapaxes(v_pg, -1, -2)
    return (y_prompt, y_sample, k_p[None], v_p[None], s_prompt[None], k_s[None], v_s[None], s_sample[None])
```

```python
import functools

import numpy as np
import jax
import jax.numpy as jnp
from jax import lax
from jax.experimental import pallas as pl
from jax.experimental.pallas import tpu as pltpu

N_HEADS_A = 8
HEAD_DIM_A = 64
WIDTH_A = N_HEADS_A * HEAD_DIM_A
MOBA_BLOCK = 256
MOBA_TOPK = 3
N_HEADS_B = 4
KEY_DIM_B = 128
VAL_DIM_B = 128
WIDTH_B = N_HEADS_B * VAL_DIM_B
HGRN_CHUNK = 32
NORM_EPS = 1e-6

LANES = 128
MASK_LANES = LANES
MAX_BLOCKS = MASK_LANES // N_HEADS_A
NEG = -(2.0 ** 30)
SLOPE_LANE = HEAD_DIM_A
SLOPE_PARTS = 3
LOG2E = 1.4426950408889634
BF16_ROWS = 16
V_ROWS = HEAD_DIM_A + BF16_ROWS
MOBA_HEADS = 4
MOBA_GROUP = 2
VMEM_LIMIT = 48 * 1024 * 1024

F32 = jnp.float32
BF = jnp.bfloat16
NT = (((1,), (1,)), ((), ()))
TN = (((0,), (0,)), ((), ()))


def _alibi_slopes(n):
    return [2.0 ** (-8.0 * (i + 1) / n) for i in range(n)]


def _rms(x, g):
    ms = jnp.mean(x * x, axis=-1, keepdims=True)
    return x * lax.rsqrt(ms + NORM_EPS) * g


def _split2(a):
    hi = a.astype(BF)
    lo = (a - hi.astype(F32)).astype(BF)
    return hi, lo


def _split3(a):
    hi = a.astype(BF)
    r = a - hi.astype(F32)
    mid = r.astype(BF)
    lo = (r - mid.astype(F32)).astype(BF)
    return hi, mid, lo


def _dot(a, b):
    return jnp.dot(a, b, preferred_element_type=F32)


def _dot_nt(a, b):
    return lax.dot_general(a, b, NT, preferred_element_type=F32)


def _dot3_nt(a, b):
    ah, al = _split2(a)
    bh, bl = _split2(b)
    return _dot_nt(ah, bh) + _dot_nt(ah, bl) + _dot_nt(al, bh)


def _sigmoid(x):
    return 1.0 / (1.0 + jnp.exp(-x))


def _silu(x):
    return x * _sigmoid(x)


def _params(*sem):
    return pltpu.CompilerParams(dimension_semantics=sem, vmem_limit_bytes=VMEM_LIMIT)


def _topk_blocks(gate, valid):
    x = jnp.where(valid, gate, -jnp.inf)
    blk1 = lax.broadcasted_iota(jnp.int32, (1, MASK_LANES), 1) // N_HEADS_A
    rank = jnp.zeros(gate.shape, F32)
    for kb in range(1, MAX_BLOCKS):
        y = pltpu.roll(x, kb * N_HEADS_A, axis=1)
        lower_index = jnp.where(blk1 >= kb, 1.0, 0.0)
        rank = rank + jnp.where(y > x, 1.0, jnp.where(y == x, lower_index, 0.0))
    return jnp.where(valid, jnp.where(rank < MOBA_TOPK, 1.0, 0.0), 0.0)


def _proj_prompt_kernel(x_ref, g_ref, w_ref, qext_ref, qh_ref, kh_ref, mb_ref, vt_ref,
                        kpg_ref, vpg_ref, qb_ref, fb_ref, ib_ref, km_ref):
    i = pl.program_id(1)
    t = x_ref.shape[1]
    page = kpg_ref.shape[4]
    @pl.when(i == 0)
    def _():
        km_ref[...] = jnp.zeros(km_ref.shape, F32)

    hn = _rms(x_ref[0], g_ref[...]).astype(BF)

    def proj(c):
        return _dot(hn, w_ref[:, c * WIDTH_A:(c + 1) * WIDTH_A])

    lane = lax.broadcasted_iota(jnp.int32, (t, LANES), 1)
    row = lax.broadcasted_iota(jnp.int32, (t, LANES), 0)
    q = proj(0)
    gate = _dot3_nt(q, km_ref[...])
    blk = lane // N_HEADS_A
    sel = _topk_blocks(gate, blk < i)
    mb_ref[0] = jnp.where(sel > 0.0, 0.0, NEG).astype(BF)

    k = proj(1)
    kmean = jnp.sum(k, axis=0, keepdims=True) / t
    hrow = lax.broadcasted_iota(jnp.int32, (N_HEADS_A, WIDTH_A), 0)
    hcol = lax.broadcasted_iota(jnp.int32, (N_HEADS_A, WIDTH_A), 1) // HEAD_DIM_A
    km_ref[pl.ds(pl.multiple_of(i * N_HEADS_A, N_HEADS_A), N_HEADS_A), :] = jnp.where(
        hrow == hcol, jnp.broadcast_to(kmean, (N_HEADS_A, WIDTH_A)), 0.0)

    low = lane < HEAD_DIM_A
    kext = jnp.where((lane >= SLOPE_LANE) & (lane < SLOPE_LANE + SLOPE_PARTS), row.astype(F32), 0.0)
    scale = HEAD_DIM_A ** -0.5 * LOG2E
    for hp in range(N_HEADS_A // 2):
        cols = slice(hp * LANES, (hp + 1) * LANES)
        qp = q[:, cols] * scale
        kp = k[:, cols]
        qpr = pltpu.roll(qp, HEAD_DIM_A, axis=1)
        kpr = pltpu.roll(kp, HEAD_DIM_A, axis=1)
        qh_ref[0, 2 * hp] = jnp.where(low, qp, qext_ref[2 * hp:2 * hp + 1, :]).astype(BF)
        qh_ref[0, 2 * hp + 1] = jnp.where(low, qpr, qext_ref[2 * hp + 1:2 * hp + 2, :]).astype(BF)
        kh_ref[0, 2 * hp, 0] = jnp.where(low, kp, kext).astype(BF)
        kh_ref[0, 2 * hp + 1, 0] = jnp.where(low, kpr, kext).astype(BF)

    v = proj(2)
    kt = k.T
    vt = v.T
    ones_row = (lax.broadcasted_iota(jnp.int32, (BF16_ROWS, t), 0) == 0).astype(F32)
    for h in range(N_HEADS_A):
        feat = slice(h * HEAD_DIM_A, (h + 1) * HEAD_DIM_A)
        vt_ref[0, h, 0] = jnp.concatenate([vt[feat, :], ones_row], axis=0).astype(BF)
        for p in range(t // page):
            toks = slice(p * page, (p + 1) * page)
            kpg_ref[0, p, h] = kt[feat, toks]
            vpg_ref[0, p, h] = vt[feat, toks]

    qb_ref[0] = proj(3)
    fb_ref[0] = proj(4)
    ib_ref[0] = proj(5)


def _proj_prompt(x, g_in, w1, qext, page):
    b, s, d = x.shape
    t = MOBA_BLOCK
    nblk = s // t
    nproj = w1.shape[1]
    out_shape = (
        jax.ShapeDtypeStruct((b, N_HEADS_A, s, LANES), BF),
        jax.ShapeDtypeStruct((b, N_HEADS_A, nblk, t, LANES), BF),
        jax.ShapeDtypeStruct((b, s, MASK_LANES), BF),
        jax.ShapeDtypeStruct((b, N_HEADS_A, nblk, V_ROWS, t), BF),
        jax.ShapeDtypeStruct((b, s // page, N_HEADS_A, HEAD_DIM_A, page), F32),
        jax.ShapeDtypeStruct((b, s // page, N_HEADS_A, HEAD_DIM_A, page), F32),
        jax.ShapeDtypeStruct((b, s, WIDTH_B), F32),
        jax.ShapeDtypeStruct((b, s, WIDTH_B), F32),
        jax.ShapeDtypeStruct((b, s, WIDTH_B), F32),
    )
    pg = pl.BlockSpec((1, t // page, N_HEADS_A, HEAD_DIM_A, page), lambda bi, i: (bi, i, 0, 0, 0))
    row3 = lambda w: pl.BlockSpec((1, t, w), lambda bi, i: (bi, i, 0))
    return pl.pallas_call(
        _proj_prompt_kernel,
        grid=(b, nblk),
        in_specs=[
            row3(d),
            pl.BlockSpec((1, d), lambda bi, i: (0, 0)),
            pl.BlockSpec((d, nproj), lambda bi, i: (0, 0)),
            pl.BlockSpec((N_HEADS_A, LANES), lambda bi, i: (0, 0)),
        ],
        out_specs=(
            pl.BlockSpec((1, N_HEADS_A, t, LANES), lambda bi, i: (bi, 0, i, 0)),
            pl.BlockSpec((1, N_HEADS_A, 1, t, LANES), lambda bi, i: (bi, 0, i, 0, 0)),
            row3(MASK_LANES),
            pl.BlockSpec((1, N_HEADS_A, 1, V_ROWS, t), lambda bi, i: (bi, 0, i, 0, 0)),
            pg, pg, row3(WIDTH_B), row3(WIDTH_B), row3(WIDTH_B),
        ),
        out_shape=out_shape,
        scratch_shapes=[pltpu.VMEM((MASK_LANES, WIDTH_A), F32)],
        compiler_params=_params("arbitrary", "arbitrary"),
        name="proj_prompt",
    )(x, g_in, w1, qext)


def _moba_kernel(slopes_ref, pt_ref, qh_ref, mb_ref, kh_ref, vt_ref, oh_ref, ck_ref, o_ref, bs_ref, pbuf, psem):
    hg = pl.program_id(1)
    qi = pl.program_id(2)
    step = (pl.program_id(0) * pl.num_programs(1) + hg) * pl.num_programs(2) + qi
    nsteps = pl.num_programs(0) * pl.num_programs(1) * pl.num_programs(2)
    _page_sums(step, nsteps, pt_ref, ck_ref, bs_ref, pbuf, psem)

    t = mb_ref.shape[1]
    grp = MOBA_GROUP
    mb = mb_ref[0]
    krow = lax.broadcasted_iota(jnp.int32, (t, t), 0)
    qcol = lax.broadcasted_iota(jnp.int32, (t, t), 1)
    causal = krow <= qcol
    heads = range(qh_ref.shape[1])
    q_aug = [jnp.concatenate([qh_ref[0, hh], mb], axis=1) for hh in heads]
    slope = [slopes_ref[hg * len(heads) + hh] for hh in heads]
    no_mask = jnp.zeros((t, MASK_LANES), BF)

    s_own = [_dot_nt(jnp.concatenate([kh_ref[0, hh, qi], no_mask], axis=1), q_aug[hh]) for hh in heads]
    carry = []
    for hh in heads:
        s = jnp.where(causal, s_own[hh], NEG)
        m = jnp.max(s, axis=0, keepdims=True)
        carry.append((m, jnp.exp2(s - m).astype(BF)))
    carry = tuple((m, _dot(vt_ref[0, hh, qi], p)) for hh, (m, p) in zip(heads, carry))

    def past_group(g, carry):
        j0 = g * grp
        s_all = []
        for hh in heads:
            k_aug = jnp.concatenate([kh_ref[0, hh, pl.ds(j0, grp)].reshape(grp * t, LANES),
                                     oh_ref[hh, pl.ds(j0, grp)].reshape(grp * t, MASK_LANES)], axis=1)
            s_all.append(_dot_nt(k_aug, q_aug[hh]))
        stats = []
        for hh in heads:
            m, _ = carry[hh]
            subs = [s_all[hh][i * t:(i + 1) * t] for i in range(grp)]
            cj = [slope[hh] * ((j0 + i - qi) * t).astype(F32) for i in range(grp)]
            m_new = m
            for i in range(grp):
                m_new = jnp.maximum(m_new, jnp.max(subs[i], axis=0, keepdims=True) + cj[i])
            p = jnp.concatenate([jnp.exp2(subs[i] - (m_new - cj[i])).astype(BF) for i in range(grp)], axis=0)
            stats.append((m_new, jnp.exp2(m - m_new), p))
        out = []
        for hh in heads:
            m_new, alpha, p = stats[hh]
            vt = jnp.concatenate([vt_ref[0, hh, j0 + i] for i in range(grp)], axis=1)
            out.append((m_new, alpha * carry[hh][1] + _dot(vt, p)))
        return tuple(out)

    carry = lax.fori_loop(0, (qi + grp - 1) // grp, past_group, carry)
    o_ref[0] = jnp.concatenate(
        [(acc[:HEAD_DIM_A] / acc[HEAD_DIM_A:HEAD_DIM_A + 1]).T for (_, acc) in carry], axis=1).astype(o_ref.dtype)


def _page_sums(step, nsteps, pt_ref, ck_ref, bs_ref, pbuf, psem):
    pps = pbuf.shape[1]
    half = pps // 2

    def copy(s, slot, i):
        page = pt_ref[s * pps + (2 * i if i < half else 2 * (i - half) + 1)]
        return pltpu.make_async_copy(ck_ref.at[page], pbuf.at[slot, i], psem.at[slot])

    slot = step % 2

    @pl.when(step == 0)
    def _():
        for i in range(pps):
            copy(0, 0, i).start()

    @pl.when(step + 1 < nsteps)
    def _():
        for i in range(pps):
            copy(step + 1, 1 - slot, i).start()

    base = (step * half) % LANES

    @pl.when(base == 0)
    def _():
        bs_ref[...] = jnp.zeros(bs_ref.shape, F32)

    for i in range(pps):
        copy(step, slot, i).wait()
    lane = lax.broadcasted_iota(jnp.int32, (HEAD_DIM_A, LANES), 1)
    for h in range(N_HEADS_A):
        feat = slice(h * HEAD_DIM_A, (h + 1) * HEAD_DIM_A)
        tile = bs_ref[0, feat, :]
        for i in range(half):
            red = jnp.sum(pbuf[slot, i, h] + pbuf[slot, half + i, h], axis=1, keepdims=True)
            tile = jnp.where(lane == base + i, red, tile)
        bs_ref[0, feat, :] = tile


def _moba_prompt(slopes, pt_flat, qh, mb, kh, vt, onehot, cache_kt):
    b, _, s, _ = qh.shape
    t = MOBA_BLOCK
    nblk = s // t
    hg = MOBA_HEADS
    vrows = vt.shape[3]
    _, h, dh, page = cache_kt.shape
    ngrp = N_HEADS_A // hg
    nsteps = b * ngrp * nblk
    pps = pt_flat.shape[0] // nsteps
    half = pps // 2
    assert pps * nsteps == pt_flat.shape[0] and pps % 2 == 0 and LANES % half == 0
    bs_shape = (pl.cdiv(nsteps * half, LANES), WIDTH_A, LANES)
    return pl.pallas_call(
        _moba_kernel,
        grid_spec=pltpu.PrefetchScalarGridSpec(
            num_scalar_prefetch=2,
            grid=(b, ngrp, nblk),
            in_specs=[
                pl.BlockSpec((1, hg, t, LANES), lambda bi, g, qi, sl, pt: (bi, g, qi, 0)),
                pl.BlockSpec((1, t, MASK_LANES), lambda bi, g, qi, sl, pt: (bi, qi, 0)),
                pl.BlockSpec((1, hg, nblk, t, LANES), lambda bi, g, qi, sl, pt: (bi, g, 0, 0, 0)),
                pl.BlockSpec((1, hg, nblk, vrows, t), lambda bi, g, qi, sl, pt: (bi, g, 0, 0, 0)),
                pl.BlockSpec((hg, nblk, t, MASK_LANES), lambda bi, g, qi, sl, pt: (g, 0, 0, 0)),
                pl.BlockSpec(memory_space=pl.ANY),
            ],
            out_specs=(
                pl.BlockSpec((1, t, hg * HEAD_DIM_A), lambda bi, g, qi, sl, pt: (bi, qi, g)),
                pl.BlockSpec((1, WIDTH_A, LANES),
                             lambda bi, g, qi, sl, pt: ((((bi * ngrp + g) * nblk + qi) * half) // LANES, 0, 0)),
            ),
            scratch_shapes=[pltpu.VMEM((2, pps, h, dh, page), F32), pltpu.SemaphoreType.DMA((2,))],
        ),
        out_shape=(jax.ShapeDtypeStruct((b, s, WIDTH_A), BF), jax.ShapeDtypeStruct(bs_shape, F32)),
        compiler_params=_params("arbitrary", "arbitrary", "arbitrary"),
        name="moba_prompt",
    )(slopes, pt_flat, qh, mb, kh, vt, onehot, cache_kt)


def _lower_bound(lb_logits, layer):
    e = jnp.exp(lb_logits - jnp.max(lb_logits, axis=0, keepdims=True))
    return jnp.sum(e[:layer + 1], axis=0, keepdims=True) / jnp.sum(e, axis=0, keepdims=True)


def _hgrn_kernel(layer, sel_ref, pt_ref, lbl_ref, gn_ref, tri_ref, q_ref, f_ref, i_ref,
                 qkv_ref, bias_ref, r_ref, s0_ref, ck_ref, cv_ref,
                 o_ref, sfin_ref, oa_ref, ob_ref, sn_ref, st_ref, kbuf, vbuf, sem):
    ti = pl.program_id(1)
    step = pl.program_id(0) * pl.num_programs(1) + ti
    nsteps = pl.num_programs(0) * pl.num_programs(1)
    _sample_attention(step, nsteps, sel_ref, pt_ref, qkv_ref, bias_ref, ck_ref, cv_ref, oa_ref, kbuf, vbuf, sem)
    _hgrn_sample_step(layer, lbl_ref, gn_ref, r_ref, s0_ref, ob_ref, sn_ref)

    t = q_ref.shape[1]
    c = HGRN_CHUNK
    heads = range(N_HEADS_B)
    hk = lambda a, h: a[:, h * KEY_DIM_B:(h + 1) * KEY_DIM_B]

    @pl.when(ti == 0)
    def _():
        st_ref[...] = jnp.zeros(st_ref.shape, F32)

    lb = _lower_bound(lbl_ref[...], layer)
    f = lb + (1.0 - lb) * _sigmoid(f_ref[0])
    k = 1.0 - f
    tri = tri_ref[...]
    cum = sum(_dot(tri, part) for part in _split3(jnp.log(f)))
    tot = jnp.concatenate(
        [jnp.broadcast_to(cum[(n + 1) * c - 1:(n + 1) * c], (c, cum.shape[1])) for n in range(t // c)], axis=0)
    qd = (q_ref[0] * jnp.exp(cum)).astype(BF)
    ki = (k * jnp.exp(-cum)).astype(BF)
    kd = (k * jnp.exp(tot - cum)).astype(BF)
    vb = i_ref[0].astype(BF)
    a = [jnp.where(tri > 0, _dot_nt(hk(qd, h), hk(ki, h)), 0.0).astype(BF) for h in heads]
    o_intra = [_dot(a[h], hk(vb, h)) for h in heads]
    chunks = [slice(n * c, (n + 1) * c) for n in range(t // c)]
    upd = [[lax.dot_general(hk(vb, h)[r], hk(kd, h)[r], TN, preferred_element_type=F32) for h in heads]
           for r in chunks]
    st = [st_ref[h] for h in heads]
    st_in = []
    for n, r in enumerate(chunks):
        st_in.append([st[h].astype(BF) for h in heads])
        decay = jnp.exp(tot[r.stop - 1:r.stop])
        st = [st[h] * hk(decay, h) + upd[n][h] for h in heads]
    inter = [[_dot_nt(hk(qd, h)[r], st_in[n][h]) for h in heads] for n, r in enumerate(chunks)]
    for h in heads:
        st_ref[h] = st[h]
        gn = gn_ref[:, h * VAL_DIM_B:(h + 1) * VAL_DIM_B]
        o = o_intra[h] + jnp.concatenate([inter[n][h] for n in range(len(chunks))], axis=0)
        o_ref[0, :, h * VAL_DIM_B:(h + 1) * VAL_DIM_B] = _rms(o, gn).astype(o_ref.dtype)

    @pl.when(ti == pl.num_programs(1) - 1)
    def _():
        for h in heads:
            sfin_ref[0, h] = st[h].T


def _chunk_tri(t):
    r = np.arange(t)
    return jnp.asarray((r[:, None] // HGRN_CHUNK == r[None, :] // HGRN_CHUNK) & (r[None, :] <= r[:, None]), BF)


def _hgrn_prompt(layer, lb_logits, gnorm, qb, fb, ib, sel_flat, pt_flat, qkv_col, bias, r, state, cache_kt, cache_vt):
    b, s, _ = qb.shape
    t = 256
    nt = s // t
    ndb = r.shape[0]
    assert ndb == b * nt, "one sample sequence per hgrn_prompt grid step"
    _, h, dh, page = cache_kt.shape
    ntp = MOBA_TOPK * (MOBA_BLOCK // page)
    tok = pl.BlockSpec((1, t, WIDTH_B), lambda bi, ti, sl, pt: (bi, ti, 0))
    fixed = lambda a: pl.BlockSpec(a.shape, lambda bi, ti, sl, pt: (0,) * a.ndim)
    seq = lambda a: pl.BlockSpec((1,) + a.shape[1:], lambda bi, ti, sl, pt: (bi * nt + ti,) + (0,) * (a.ndim - 1))
    tri = _chunk_tri(t)
    r3 = r.reshape(ndb, 1, r.shape[1])
    oa_shape = jax.ShapeDtypeStruct((ndb, WIDTH_A, 1), F32)
    ob_shape = jax.ShapeDtypeStruct((ndb, 1, WIDTH_B), BF)
    sn_shape = jax.ShapeDtypeStruct(state.shape, F32)
    return pl.pallas_call(
        functools.partial(_hgrn_kernel, layer),
        grid_spec=pltpu.PrefetchScalarGridSpec(
            num_scalar_prefetch=2,
            grid=(b, nt),
            in_specs=[
                fixed(lb_logits), fixed(gnorm), fixed(tri), tok, tok, tok,
                seq(qkv_col), seq(bias), seq(r3), seq(state),
                pl.BlockSpec(memory_space=pl.ANY), pl.BlockSpec(memory_space=pl.ANY),
            ],
            out_specs=(
                tok,
                pl.BlockSpec((1, N_HEADS_B, KEY_DIM_B, VAL_DIM_B), lambda bi, ti, sl, pt: (bi, 0, 0, 0)),
                seq(oa_shape), seq(ob_shape), seq(sn_shape),
            ),
            scratch_shapes=[pltpu.VMEM((N_HEADS_B, VAL_DIM_B, KEY_DIM_B), F32),
                            pltpu.VMEM((2, ntp, h, dh, page), F32),
                            pltpu.VMEM((2, ntp, h, dh, page), F32),
                            pltpu.SemaphoreType.DMA((2, 2))],
        ),
        out_shape=(
            jax.ShapeDtypeStruct((b, s, WIDTH_B), BF),
            jax.ShapeDtypeStruct((b, N_HEADS_B, KEY_DIM_B, VAL_DIM_B), F32),
            oa_shape, ob_shape, sn_shape,
        ),
        compiler_params=_params("arbitrary", "arbitrary"),
        name="hgrn_prompt",
    )(sel_flat, pt_flat, lb_logits, gnorm, tri, qb, fb, ib, qkv_col, bias, r3, state, cache_kt, cache_vt)


def _out_kernel(x_ref, oa_ref, ob_ref, g_ref, w2_ref, wa_ref, wb_ref, wo_ref, gf_ref, y_ref):
    x = x_ref[...]
    hn = _rms(x, g_ref[...]).astype(BF)
    d = x.shape[1]

    def proj(lo, hi):
        return _dot(hn, w2_ref[:, lo:hi])

    ya = _dot((oa_ref[...].astype(F32) * _silu(proj(0, WIDTH_A))).astype(BF), wa_ref[...])
    yb = _dot((ob_ref[...].astype(F32) * _silu(proj(WIDTH_A, WIDTH_A + WIDTH_B))).astype(BF), wb_ref[...])
    g0 = WIDTH_A + WIDTH_B
    mixed = _sigmoid(proj(g0, g0 + d)) * ya + _sigmoid(proj(g0 + d, g0 + 2 * d)) * yb
    out = x + _dot(mixed.astype(BF), wo_ref[...])
    y_ref[...] = _rms(out, gf_ref[...])


def _out_proj(x2, oa, ob, g_in, w2, wa, wb, wo, g_fin, tm):
    n, d = x2.shape
    rows = lambda w: pl.BlockSpec((tm, w), lambda i: (i, 0))
    full = lambda a: pl.BlockSpec(a.shape, lambda i: (0, 0))
    return pl.pallas_call(
        _out_kernel,
        grid=(n // tm,),
        in_specs=[rows(d), rows(WIDTH_A), rows(WIDTH_B), full(g_in), full(w2), full(wa), full(wb), full(wo),
                  full(g_fin)],
        out_specs=rows(d),
        out_shape=jax.ShapeDtypeStruct((n, d), F32),
        compiler_params=_params("arbitrary"),
        name="out_proj",
    )(x2, oa, ob, g_in, w2, wa, wb, wo, g_fin)


def _proj_sample_kernel(x_ref, g_ref, w_ref, r_ref):
    hn = _rms(x_ref[...], g_ref[...]).astype(BF)
    for c in range(w_ref.shape[1] // WIDTH_A):
        cols = slice(c * WIDTH_A, (c + 1) * WIDTH_A)
        r_ref[:, cols] = _dot(hn, w_ref[:, cols])


def _proj_sample(x2, g_in, w1):
    n, _ = x2.shape
    return pl.pallas_call(
        _proj_sample_kernel,
        out_shape=jax.ShapeDtypeStruct((n, w1.shape[1]), F32),
        compiler_params=pltpu.CompilerParams(vmem_limit_bytes=VMEM_LIMIT),
        name="proj_sample",
    )(x2, g_in, w1)


def _sample_gate_kernel(past, slope_ref, r_ref, bs_ref, sel_ref, bias_ref):
    dpg = r_ref.shape[1]
    nblk = LANES // dpg
    page = bias_ref.shape[3]
    ppb = MOBA_BLOCK // page
    hrow = lax.broadcasted_iota(jnp.int32, (N_HEADS_A, WIDTH_A), 0)
    hcol = lax.broadcasted_iota(jnp.int32, (N_HEADS_A, WIDTH_A), 1) // HEAD_DIM_A
    lane = lax.broadcasted_iota(jnp.int32, (N_HEADS_A, LANES), 1)
    tok = lax.broadcasted_iota(jnp.int32, (N_HEADS_A, page), 1)
    kh, kl = _split2(bs_ref[0] / MOBA_BLOCK)
    for j in range(dpg):
        qbd = jnp.where(hrow == hcol, jnp.broadcast_to(r_ref[0, j:j + 1, :WIDTH_A], (N_HEADS_A, WIDTH_A)), 0.0)
        qh, ql = _split2(qbd)
        gate = _dot(qh, kh) + _dot(qh, kl) + _dot(ql, kh)
        blk = lane - j * nblk
        x = jnp.where((blk >= 0) & (blk < nblk), gate, -jnp.inf)
        out = jnp.zeros((N_HEADS_A, LANES), jnp.int32)
        for rank in range(MOBA_TOPK):
            m = jnp.max(x, axis=1, keepdims=True)
            first = jnp.min(jnp.where(x == m, blk, LANES), axis=1, keepdims=True)
            out = jnp.where(lane == rank, first, out)
            x = jnp.where(blk == first, -jnp.inf, x)
            for p in range(ppb):
                dist = past - (first * MOBA_BLOCK + p * page) - tok
                bias_ref[j, rank * ppb + p] = -slope_ref[...] * dist.astype(F32)
        for unused in range(MOBA_TOPK * ppb, bias_ref.shape[1]):
            bias_ref[j, unused] = jnp.zeros((N_HEADS_A, page), F32)
        sel_ref[j] = out


def _sample_gate(past, r, bsum_t, nblk, page):
    ndb = r.shape[0]
    dpg = LANES // nblk
    slots = pl.cdiv(MOBA_TOPK * (MOBA_BLOCK // page), 8) * 8
    slope_col = jnp.asarray(_alibi_slopes(N_HEADS_A), F32)[:, None]
    return pl.pallas_call(
        functools.partial(_sample_gate_kernel, past),
        grid=(ndb // dpg,),
        in_specs=[pl.BlockSpec((N_HEADS_A, 1), lambda i: (0, 0)),
                  pl.BlockSpec((1, dpg, r.shape[1]), lambda i: (i, 0, 0)),
                  pl.BlockSpec((1, WIDTH_A, LANES), lambda i: (i, 0, 0))],
        out_specs=(pl.BlockSpec((dpg, N_HEADS_A, LANES), lambda i: (i, 0, 0)),
                   pl.BlockSpec((dpg, slots, N_HEADS_A, page), lambda i: (i, 0, 0, 0))),
        out_shape=(jax.ShapeDtypeStruct((ndb, N_HEADS_A, LANES), jnp.int32),
                   jax.ShapeDtypeStruct((ndb, slots, N_HEADS_A, page), F32)),
        compiler_params=_params("arbitrary"),
        name="sample_gate",
    )(slope_col, r.reshape(ndb // dpg, dpg, r.shape[1]), bsum_t)


def _sample_attention(db, ndb, sel_ref, pt_ref, qkv_ref, bias_ref, ck_ref, cv_ref, o_ref, kbuf, vbuf, sem):
    npg = pt_ref.shape[0] // ndb
    ntp = kbuf.shape[1]
    ppb = ntp // MOBA_TOPK

    def copies(d, slot):
        out = []
        for h in range(N_HEADS_A):
            for tp in range(ntp):
                blk = sel_ref[(d * N_HEADS_A + h) * MOBA_TOPK + tp // ppb]
                pg = pt_ref[d * npg + blk * ppb + tp % ppb]
                out.append(pltpu.make_async_copy(ck_ref.at[pg, h], kbuf.at[slot, tp, h], sem.at[0, slot]))
                out.append(pltpu.make_async_copy(cv_ref.at[pg, h], vbuf.at[slot, tp, h], sem.at[1, slot]))
        return out

    slot = db % 2

    @pl.when(db == 0)
    def _():
        for cp in copies(0, 0):
            cp.start()

    @pl.when(db + 1 < ndb)
    def _():
        for cp in copies(db + 1, 1 - slot):
            cp.start()

    for cp in copies(db, slot):
        cp.wait()

    hd = (N_HEADS_A, HEAD_DIM_A, 1)
    qkv = qkv_ref[0]
    qc = (qkv[:WIDTH_A] * HEAD_DIM_A ** -0.5).reshape(hd)
    knc = qkv[WIDTH_A:2 * WIDTH_A].reshape(hd)
    vnc = qkv[2 * WIDTH_A:].reshape(hd)
    lg_new = jnp.sum(qc * knc, axis=1)
    lg = jnp.sum(kbuf[slot] * qc[None], axis=2) + bias_ref[0, :ntp]
    m = jnp.maximum(jnp.max(jnp.max(lg, axis=0), axis=1, keepdims=True), lg_new)
    p_new = jnp.exp(lg_new - m)
    p = jnp.exp(lg - m[None])
    l = jnp.sum(jnp.sum(p, axis=0), axis=1, keepdims=True) + p_new
    pv = jnp.sum(vbuf[slot] * p[:, :, None, :], axis=0)
    acc = jnp.sum(pv, axis=2, keepdims=True) + p_new[:, :, None] * vnc
    o_ref[0] = (acc / l[:, :, None]).reshape(WIDTH_A, 1)


def _hgrn_sample_step(layer, lbl_ref, gn_ref, r_ref, s_ref, o_ref, sn_ref):
    ndb = r_ref.shape[0]
    kk = KEY_DIM_B
    eye = (lax.broadcasted_iota(jnp.int32, (kk, kk), 0) == lax.broadcasted_iota(jnp.int32, (kk, kk), 1))
    ones = jnp.ones((kk, VAL_DIM_B), BF)

    def column(vec):
        diag = jnp.where(eye, jnp.broadcast_to(vec, (kk, kk)), 0.0)
        return sum(_dot(part, ones) for part in _split3(diag))

    lb_all = _lower_bound(lbl_ref[...], layer)
    q0, f0, i0 = WIDTH_A * 3, WIDTH_A * 3 + WIDTH_B, WIDTH_A * 3 + 2 * WIDTH_B
    for db in range(ndb):
        row = r_ref[db]
        outs = []
        for h in range(N_HEADS_B):
            cols = slice(h * kk, (h + 1) * kk)
            lb = lb_all[:, cols]
            q = row[:, q0 + h * kk:q0 + (h + 1) * kk]
            f = lb + (1.0 - lb) * _sigmoid(row[:, f0 + h * kk:f0 + (h + 1) * kk])
            v = row[:, i0 + h * VAL_DIM_B:i0 + (h + 1) * VAL_DIM_B]
            fcol = column(f)
            s_new = fcol * s_ref[db, h] + (1.0 - fcol) * v
            sn_ref[db, h] = s_new
            o = jnp.sum(column(q) * s_new, axis=0, keepdims=True)
            outs.append(_rms(o, gn_ref[:, h * VAL_DIM_B:(h + 1) * VAL_DIM_B]))
        o_ref[db] = jnp.concatenate(outs, axis=1).astype(o_ref.dtype)


def _q_extension():
    ext = np.zeros((N_HEADS_A, LANES), np.float32)
    for h, slope in enumerate(_alibi_slopes(N_HEADS_A)):
        rest = np.float32(slope * LOG2E)
        for part in range(SLOPE_PARTS):
            piece = np.float32(rest.astype(jnp.bfloat16))
            ext[h, SLOPE_LANE + part] = piece
            rest = np.float32(rest - piece)
    return jnp.asarray(ext)


def _block_onehot(nblk, t):
    lane = np.arange(MASK_LANES)[None, None, :]
    tgt = (np.arange(nblk)[None, :, None] * N_HEADS_A + np.arange(N_HEADS_A)[:, None, None])
    oh = (lane == tgt).astype(np.float32)
    return jnp.broadcast_to(jnp.asarray(oh, BF)[:, :, None, :], (N_HEADS_A, nblk, t, MASK_LANES))


def kernel(x_prompt, x_sample, cache_k, cache_v, state_hgrn, page_table, norm_in_g, w_in,
           hgrn_lb_logits, hgrn_norm_g, w_branch_a, w_branch_b, w_out, final_norm_g):
    b, s, d = x_prompt.shape
    ndb, ds, _ = x_sample.shape
    depth, n_pool, _, page, _ = cache_k.shape
    npg = page_table.shape[1]
    past = npg * page
    ppb = MOBA_BLOCK // page
    assert depth == 1 and ds == 1
    assert s % (MOBA_BLOCK * MOBA_GROUP) == 0 and s // MOBA_BLOCK <= MAX_BLOCKS
    assert past % MOBA_BLOCK == 0 and past // MOBA_BLOCK >= MOBA_TOPK
    assert ppb == 2
    nblk_s = npg // ppb
    assert LANES % nblk_s == 0 and ndb % (LANES // nblk_s) == 0
    layer = 0

    wk = N_HEADS_B * KEY_DIM_B
    a3 = 3 * WIDTH_A
    wl = w_in[layer]
    w1 = jnp.concatenate([wl[:, :a3], wl[:, a3 + WIDTH_A:a3 + WIDTH_A + 2 * wk + WIDTH_B]], axis=1).astype(BF)
    zb0 = a3 + WIDTH_A + 2 * wk + WIDTH_B
    w2 = jnp.concatenate([wl[:, a3:a3 + WIDTH_A], wl[:, zb0:]], axis=1).astype(BF)
    wa = w_branch_a[layer].astype(BF)
    wb = w_branch_b[layer].astype(BF)
    wo = w_out[layer].astype(BF)
    g_in = norm_in_g[layer][None, :]
    g_fin = final_norm_g[None, :]
    gnorm = hgrn_norm_g[layer][None, :]
    slopes = jnp.asarray([m * LOG2E for m in _alibi_slopes(N_HEADS_A)], F32)

    pt_flat = page_table.reshape(ndb * npg)
    ckt = jnp.swapaxes(cache_k[layer], -1, -2)
    cvt = jnp.swapaxes(cache_v[layer], -1, -2)

    xs = x_sample.reshape(ndb, d)
    r = _proj_sample(xs, g_in, w1)
    qh, kh, mb, vt, k_pg, v_pg, qb, fb, ib = _proj_prompt(x_prompt, g_in, w1, _q_extension(), page)
    o_a, bsum_t = _moba_prompt(slopes, pt_flat, qh, mb, kh, vt, _block_onehot(s // MOBA_BLOCK, MOBA_BLOCK), ckt)
    sel, bias = _sample_gate(past, r, bsum_t, nblk_s, page)
    sel = sel[:, :, :MOBA_TOPK].reshape(ndb * N_HEADS_A * MOBA_TOPK)
    qkv_col = r[:, :3 * WIDTH_A].reshape(ndb, 3 * WIDTH_A, 1)
    o_b, s_prompt, oa_s, ob_s, s_sample = _hgrn_prompt(
        layer, hgrn_lb_logits, gnorm, qb, fb, ib, sel, pt_flat, qkv_col, bias, r, state_hgrn[layer], ckt, cvt)
    y_prompt = _out_proj(x_prompt.reshape(b * s, d), o_a.reshape(b * s, WIDTH_A), o_b.reshape(b * s, WIDTH_B),
                         g_in, w2, wa, wb, wo, g_fin, 1024).reshape(b, s, d)
    y_sample = _out_proj(xs, oa_s.reshape(ndb, WIDTH_A).astype(BF), ob_s.reshape(ndb, WIDTH_B),
                         g_in, w2, wa, wb, wo, g_fin, ndb).reshape(ndb, 1, d)

    k_s = r[:, WIDTH_A:2 * WIDTH_A].reshape(ndb, N_HEADS_A, 1, HEAD_DIM_A)
    v_s = r[:, 2 * WIDTH_A:3 * WIDTH_A].reshape(ndb, N_HEADS_A, 1, HEAD_DIM_A)
    k_p = jnp.swapaxes(k_pg, -1, -2)
    v_p = jnp.swapaxes(v_pg, -1, -2)
    return (y_prompt, y_sample, k_p[None], v_p[None], s_prompt[None], k_s[None], v_s[None], s_sample[None])
```

```python
import functools

import numpy as np
import jax
import jax.numpy as jnp
from jax import lax
from jax.experimental import pallas as pl
from jax.experimental.pallas import tpu as pltpu

N_HEADS_A = 8
HEAD_DIM_A = 64
WIDTH_A = N_HEADS_A * HEAD_DIM_A
MOBA_BLOCK = 256
MOBA_TOPK = 3
N_HEADS_B = 4
KEY_DIM_B = 128
VAL_DIM_B = 128
WIDTH_B = N_HEADS_B * VAL_DIM_B
HGRN_CHUNK = 32
NORM_EPS = 1e-6

LANES = 128
MASK_LANES = LANES
MAX_BLOCKS = MASK_LANES // N_HEADS_A
NEG = -(2.0 ** 30)
SLOPE_LANE = HEAD_DIM_A
SLOPE_PARTS = 3
LOG2E = 1.4426950408889634
BF16_ROWS = 16
V_ROWS = HEAD_DIM_A + BF16_ROWS
MOBA_HEADS = 4
MOBA_GROUP = (4, 2)
VMEM_LIMIT = 48 * 1024 * 1024

F32 = jnp.float32
BF = jnp.bfloat16
NT = (((1,), (1,)), ((), ()))
TN = (((0,), (0,)), ((), ()))


def _alibi_slopes(n):
    return [2.0 ** (-8.0 * (i + 1) / n) for i in range(n)]


def _rms(x, g):
    ms = jnp.mean(x * x, axis=-1, keepdims=True)
    return x * lax.rsqrt(ms + NORM_EPS) * g


def _split2(a):
    hi = a.astype(BF)
    lo = (a - hi.astype(F32)).astype(BF)
    return hi, lo


def _split3(a):
    hi = a.astype(BF)
    r = a - hi.astype(F32)
    mid = r.astype(BF)
    lo = (r - mid.astype(F32)).astype(BF)
    return hi, mid, lo


def _dot(a, b):
    return jnp.dot(a, b, preferred_element_type=F32)


def _dot_nt(a, b):
    return lax.dot_general(a, b, NT, preferred_element_type=F32)


def _dot3_nt(a, b):
    ah, al = _split2(a)
    bh, bl = _split2(b)
    return _dot_nt(ah, bh) + _dot_nt(ah, bl) + _dot_nt(al, bh)


def _sigmoid(x):
    return 1.0 / (1.0 + jnp.exp(-x))


def _silu(x):
    return x * _sigmoid(x)


def _params(*sem):
    return pltpu.CompilerParams(dimension_semantics=sem, vmem_limit_bytes=VMEM_LIMIT)


def _topk_blocks(gate, valid):
    x = jnp.where(valid, gate, -jnp.inf)
    blk1 = lax.broadcasted_iota(jnp.int32, (1, MASK_LANES), 1) // N_HEADS_A
    rank = jnp.zeros(gate.shape, F32)
    for kb in range(1, MAX_BLOCKS):
        y = pltpu.roll(x, kb * N_HEADS_A, axis=1)
        lower_index = jnp.where(blk1 >= kb, 1.0, 0.0)
        rank = rank + jnp.where(y > x, 1.0, jnp.where(y == x, lower_index, 0.0))
    return jnp.where(valid, jnp.where(rank < MOBA_TOPK, 1.0, 0.0), 0.0)


def _proj_prompt_kernel(x_ref, g_ref, w_ref, qext_ref, qh_ref, kh_ref, mb_ref, vt_ref,
                        kpg_ref, vpg_ref, qb_ref, fb_ref, ib_ref, km_ref):
    i = pl.program_id(1)
    t = x_ref.shape[1]
    page = kpg_ref.shape[4]
    @pl.when(i == 0)
    def _():
        km_ref[...] = jnp.zeros(km_ref.shape, F32)

    hn = _rms(x_ref[0], g_ref[...]).astype(BF)

    def proj(c):
        return _dot(hn, w_ref[:, c * WIDTH_A:(c + 1) * WIDTH_A])

    lane = lax.broadcasted_iota(jnp.int32, (t, LANES), 1)
    row = lax.broadcasted_iota(jnp.int32, (t, LANES), 0)
    q = proj(0)
    gate = _dot3_nt(q, km_ref[...])
    blk = lane // N_HEADS_A
    sel = _topk_blocks(gate, blk < i)
    mb_ref[0] = jnp.where(sel > 0.0, 0.0, NEG).astype(BF)

    k = proj(1)
    kmean = jnp.sum(k, axis=0, keepdims=True) / t
    hrow = lax.broadcasted_iota(jnp.int32, (N_HEADS_A, WIDTH_A), 0)
    hcol = lax.broadcasted_iota(jnp.int32, (N_HEADS_A, WIDTH_A), 1) // HEAD_DIM_A
    km_ref[pl.ds(pl.multiple_of(i * N_HEADS_A, N_HEADS_A), N_HEADS_A), :] = jnp.where(
        hrow == hcol, jnp.broadcast_to(kmean, (N_HEADS_A, WIDTH_A)), 0.0)

    low = lane < HEAD_DIM_A
    kext = jnp.where((lane >= SLOPE_LANE) & (lane < SLOPE_LANE + SLOPE_PARTS), row.astype(F32), 0.0)
    scale = HEAD_DIM_A ** -0.5 * LOG2E
    for hp in range(N_HEADS_A // 2):
        cols = slice(hp * LANES, (hp + 1) * LANES)
        qp = q[:, cols] * scale
        kp = k[:, cols]
        qpr = pltpu.roll(qp, HEAD_DIM_A, axis=1)
        kpr = pltpu.roll(kp, HEAD_DIM_A, axis=1)
        qh_ref[0, 2 * hp] = jnp.where(low, qp, qext_ref[2 * hp:2 * hp + 1, :]).astype(BF)
        qh_ref[0, 2 * hp + 1] = jnp.where(low, qpr, qext_ref[2 * hp + 1:2 * hp + 2, :]).astype(BF)
        kh_ref[0, 2 * hp, 0] = jnp.where(low, kp, kext).astype(BF)
        kh_ref[0, 2 * hp + 1, 0] = jnp.where(low, kpr, kext).astype(BF)

    v = proj(2)
    kt = k.T
    vt = v.T
    ones_row = (lax.broadcasted_iota(jnp.int32, (BF16_ROWS, t), 0) == 0).astype(F32)
    for h in range(N_HEADS_A):
        feat = slice(h * HEAD_DIM_A, (h + 1) * HEAD_DIM_A)
        vt_ref[0, h, 0] = jnp.concatenate([vt[feat, :], ones_row], axis=0).astype(BF)
        for p in range(t // page):
            toks = slice(p * page, (p + 1) * page)
            kpg_ref[0, p, h] = kt[feat, toks]
            vpg_ref[0, p, h] = vt[feat, toks]

    qb_ref[0] = proj(3)
    fb_ref[0] = proj(4)
    ib_ref[0] = proj(5)


def _proj_prompt(x, g_in, w1, qext, page):
    b, s, d = x.shape
    t = MOBA_BLOCK
    nblk = s // t
    nproj = w1.shape[1]
    out_shape = (
        jax.ShapeDtypeStruct((b, N_HEADS_A, s, LANES), BF),
        jax.ShapeDtypeStruct((b, N_HEADS_A, nblk, t, LANES), BF),
        jax.ShapeDtypeStruct((b, s, MASK_LANES), BF),
        jax.ShapeDtypeStruct((b, N_HEADS_A, nblk, V_ROWS, t), BF),
        jax.ShapeDtypeStruct((b, s // page, N_HEADS_A, HEAD_DIM_A, page), F32),
        jax.ShapeDtypeStruct((b, s // page, N_HEADS_A, HEAD_DIM_A, page), F32),
        jax.ShapeDtypeStruct((b, s, WIDTH_B), F32),
        jax.ShapeDtypeStruct((b, s, WIDTH_B), F32),
        jax.ShapeDtypeStruct((b, s, WIDTH_B), F32),
    )
    pg = pl.BlockSpec((1, t // page, N_HEADS_A, HEAD_DIM_A, page), lambda bi, i: (bi, i, 0, 0, 0))
    row3 = lambda w: pl.BlockSpec((1, t, w), lambda bi, i: (bi, i, 0))
    return pl.pallas_call(
        _proj_prompt_kernel,
        grid=(b, nblk),
        in_specs=[
            row3(d),
            pl.BlockSpec((1, d), lambda bi, i: (0, 0)),
            pl.BlockSpec((d, nproj), lambda bi, i: (0, 0)),
            pl.BlockSpec((N_HEADS_A, LANES), lambda bi, i: (0, 0)),
        ],
        out_specs=(
            pl.BlockSpec((1, N_HEADS_A, t, LANES), lambda bi, i: (bi, 0, i, 0)),
            pl.BlockSpec((1, N_HEADS_A, 1, t, LANES), lambda bi, i: (bi, 0, i, 0, 0)),
            row3(MASK_LANES),
            pl.BlockSpec((1, N_HEADS_A, 1, V_ROWS, t), lambda bi, i: (bi, 0, i, 0, 0)),
            pg, pg, row3(WIDTH_B), row3(WIDTH_B), row3(WIDTH_B),
        ),
        out_shape=out_shape,
        scratch_shapes=[pltpu.VMEM((MASK_LANES, WIDTH_A), F32)],
        compiler_params=_params("arbitrary", "arbitrary"),
        name="proj_prompt",
    )(x, g_in, w1, qext)


def _moba_kernel(slopes_ref, pt_ref, qh_ref, mb_ref, kh_ref, vt_ref, oh_ref, ck_ref, o_ref, bs_ref, pbuf, psem):
    hg = pl.program_id(1)
    qi = pl.program_id(2)
    step = (pl.program_id(0) * pl.num_programs(1) + hg) * pl.num_programs(2) + qi
    nsteps = pl.num_programs(0) * pl.num_programs(1) * pl.num_programs(2)
    _page_sums(step, nsteps, pt_ref, ck_ref, bs_ref, pbuf, psem)

    t = mb_ref.shape[1]
    mb = mb_ref[0]
    krow = lax.broadcasted_iota(jnp.int32, (t, t), 0)
    qcol = lax.broadcasted_iota(jnp.int32, (t, t), 1)
    causal = krow <= qcol
    heads = range(qh_ref.shape[1])
    q_aug = [jnp.concatenate([qh_ref[0, hh], mb], axis=1) for hh in heads]
    slope = [slopes_ref[hg * len(heads) + hh] for hh in heads]
    no_mask = jnp.zeros((t, MASK_LANES), BF)

    s_own = [_dot_nt(jnp.concatenate([kh_ref[0, hh, qi], no_mask], axis=1), q_aug[hh]) for hh in heads]
    carry = []
    for hh in heads:
        s = jnp.where(causal, s_own[hh], NEG)
        m = jnp.max(s, axis=0, keepdims=True)
        carry.append((m, jnp.exp2(s - m).astype(BF)))
    carry = tuple((m, _dot(vt_ref[0, hh, qi], p)) for hh, (m, p) in zip(heads, carry))

    def past_group(grp, first, g, carry):
        j0 = first + g * grp
        s_all = []
        for hh in heads:
            k_aug = jnp.concatenate([kh_ref[0, hh, pl.ds(j0, grp)].reshape(grp * t, LANES),
                                     oh_ref[hh, pl.ds(j0, grp)].reshape(grp * t, MASK_LANES)], axis=1)
            s_all.append(_dot_nt(k_aug, q_aug[hh]))
        stats = []
        for hh in heads:
            m, _ = carry[hh]
            subs = [s_all[hh][i * t:(i + 1) * t] for i in range(grp)]
            cj = [slope[hh] * ((j0 + i - qi) * t).astype(F32) for i in range(grp)]
            m_new = m
            for i in range(grp):
                m_new = jnp.maximum(m_new, jnp.max(subs[i], axis=0, keepdims=True) + cj[i])
            p = jnp.concatenate([jnp.exp2(subs[i] - (m_new - cj[i])).astype(BF) for i in range(grp)], axis=0)
            stats.append((m_new, jnp.exp2(m - m_new), p))
        out = []
        for hh in heads:
            m_new, alpha, p = stats[hh]
            vt = jnp.concatenate([vt_ref[0, hh, j0 + i] for i in range(grp)], axis=1)
            out.append((m_new, alpha * carry[hh][1] + _dot(vt, p)))
        return tuple(out)

    wide, narrow = MOBA_GROUP
    n_wide = qi // wide
    carry = lax.fori_loop(0, n_wide, functools.partial(past_group, wide, 0), carry)
    n_narrow = (qi - n_wide * wide + narrow - 1) // narrow
    carry = lax.fori_loop(0, n_narrow, functools.partial(past_group, narrow, n_wide * wide), carry)
    o_ref[0] = jnp.concatenate(
        [(acc[:HEAD_DIM_A] / acc[HEAD_DIM_A:HEAD_DIM_A + 1]).T for (_, acc) in carry], axis=1).astype(o_ref.dtype)


def _page_sums(step, nsteps, pt_ref, ck_ref, bs_ref, pbuf, psem):
    pps = pbuf.shape[1]
    half = pps // 2

    def copy(s, slot, i):
        page = pt_ref[s * pps + (2 * i if i < half else 2 * (i - half) + 1)]
        return pltpu.make_async_copy(ck_ref.at[page], pbuf.at[slot, i], psem.at[slot])

    slot = step % 2

    @pl.when(step == 0)
    def _():
        for i in range(pps):
            copy(0, 0, i).start()

    @pl.when(step + 1 < nsteps)
    def _():
        for i in range(pps):
            copy(step + 1, 1 - slot, i).start()

    base = (step * half) % LANES

    @pl.when(base == 0)
    def _():
        bs_ref[...] = jnp.zeros(bs_ref.shape, F32)

    for i in range(pps):
        copy(step, slot, i).wait()
    lane = lax.broadcasted_iota(jnp.int32, (HEAD_DIM_A, LANES), 1)
    for h in range(N_HEADS_A):
        feat = slice(h * HEAD_DIM_A, (h + 1) * HEAD_DIM_A)
        tile = bs_ref[0, feat, :]
        for i in range(half):
            red = jnp.sum(pbuf[slot, i, h] + pbuf[slot, half + i, h], axis=1, keepdims=True)
            tile = jnp.where(lane == base + i, red, tile)
        bs_ref[0, feat, :] = tile


def _moba_prompt(slopes, pt_flat, qh, mb, kh, vt, onehot, cache_kt):
    b, _, s, _ = qh.shape
    t = MOBA_BLOCK
    nblk = s // t
    hg = MOBA_HEADS
    vrows = vt.shape[3]
    _, h, dh, page = cache_kt.shape
    ngrp = N_HEADS_A // hg
    nsteps = b * ngrp * nblk
    pps = pt_flat.shape[0] // nsteps
    half = pps // 2
    assert pps * nsteps == pt_flat.shape[0] and pps % 2 == 0 and LANES % half == 0
    bs_shape = (pl.cdiv(nsteps * half, LANES), WIDTH_A, LANES)
    return pl.pallas_call(
        _moba_kernel,
        grid_spec=pltpu.PrefetchScalarGridSpec(
            num_scalar_prefetch=2,
            grid=(b, ngrp, nblk),
            in_specs=[
                pl.BlockSpec((1, hg, t, LANES), lambda bi, g, qi, sl, pt: (bi, g, qi, 0)),
                pl.BlockSpec((1, t, MASK_LANES), lambda bi, g, qi, sl, pt: (bi, qi, 0)),
                pl.BlockSpec((1, hg, nblk, t, LANES), lambda bi, g, qi, sl, pt: (bi, g, 0, 0, 0)),
                pl.BlockSpec((1, hg, nblk, vrows, t), lambda bi, g, qi, sl, pt: (bi, g, 0, 0, 0)),
                pl.BlockSpec((hg, nblk, t, MASK_LANES), lambda bi, g, qi, sl, pt: (g, 0, 0, 0)),
                pl.BlockSpec(memory_space=pl.ANY),
            ],
            out_specs=(
                pl.BlockSpec((1, t, hg * HEAD_DIM_A), lambda bi, g, qi, sl, pt: (bi, qi, g)),
                pl.BlockSpec((1, WIDTH_A, LANES),
                             lambda bi, g, qi, sl, pt: ((((bi * ngrp + g) * nblk + qi) * half) // LANES, 0, 0)),
            ),
            scratch_shapes=[pltpu.VMEM((2, pps, h, dh, page), F32), pltpu.SemaphoreType.DMA((2,))],
        ),
        out_shape=(jax.ShapeDtypeStruct((b, s, WIDTH_A), BF), jax.ShapeDtypeStruct(bs_shape, F32)),
        compiler_params=_params("arbitrary", "arbitrary", "arbitrary"),
        name="moba_prompt",
    )(slopes, pt_flat, qh, mb, kh, vt, onehot, cache_kt)


def _lower_bound(lb_logits, layer):
    e = jnp.exp(lb_logits - jnp.max(lb_logits, axis=0, keepdims=True))
    return jnp.sum(e[:layer + 1], axis=0, keepdims=True) / jnp.sum(e, axis=0, keepdims=True)


def _hgrn_kernel(layer, sel_ref, pt_ref, lbl_ref, gn_ref, tri_ref, q_ref, f_ref, i_ref,
                 qkv_ref, bias_ref, r_ref, s0_ref, ck_ref, cv_ref,
                 o_ref, sfin_ref, oa_ref, ob_ref, sn_ref, st_ref, kbuf, vbuf, sem):
    ti = pl.program_id(1)
    step = pl.program_id(0) * pl.num_programs(1) + ti
    nsteps = pl.num_programs(0) * pl.num_programs(1)
    _sample_attention(step, nsteps, sel_ref, pt_ref, qkv_ref, bias_ref, ck_ref, cv_ref, oa_ref, kbuf, vbuf, sem)
    _hgrn_sample_step(layer, lbl_ref, gn_ref, r_ref, s0_ref, ob_ref, sn_ref)

    t = q_ref.shape[1]
    c = HGRN_CHUNK
    heads = range(N_HEADS_B)
    hk = lambda a, h: a[:, h * KEY_DIM_B:(h + 1) * KEY_DIM_B]

    @pl.when(ti == 0)
    def _():
        st_ref[...] = jnp.zeros(st_ref.shape, F32)

    lb = _lower_bound(lbl_ref[...], layer)
    f = lb + (1.0 - lb) * _sigmoid(f_ref[0])
    k = 1.0 - f
    tri = tri_ref[...]
    cum = sum(_dot(tri, part) for part in _split3(jnp.log(f)))
    tot = jnp.concatenate(
        [jnp.broadcast_to(cum[(n + 1) * c - 1:(n + 1) * c], (c, cum.shape[1])) for n in range(t // c)], axis=0)
    qd = (q_ref[0] * jnp.exp(cum)).astype(BF)
    ki = (k * jnp.exp(-cum)).astype(BF)
    kd = (k * jnp.exp(tot - cum)).astype(BF)
    vb = i_ref[0].astype(BF)
    a = [jnp.where(tri > 0, _dot_nt(hk(qd, h), hk(ki, h)), 0.0).astype(BF) for h in heads]
    o_intra = [_dot(a[h], hk(vb, h)) for h in heads]
    chunks = [slice(n * c, (n + 1) * c) for n in range(t // c)]
    upd = [[lax.dot_general(hk(vb, h)[r], hk(kd, h)[r], TN, preferred_element_type=F32) for h in heads]
           for r in chunks]
    st = [st_ref[h] for h in heads]
    st_in = []
    for n, r in enumerate(chunks):
        st_in.append([st[h].astype(BF) for h in heads])
        decay = jnp.exp(tot[r.stop - 1:r.stop])
        st = [st[h] * hk(decay, h) + upd[n][h] for h in heads]
    inter = [[_dot_nt(hk(qd, h)[r], st_in[n][h]) for h in heads] for n, r in enumerate(chunks)]
    for h in heads:
        st_ref[h] = st[h]
        gn = gn_ref[:, h * VAL_DIM_B:(h + 1) * VAL_DIM_B]
        o = o_intra[h] + jnp.concatenate([inter[n][h] for n in range(len(chunks))], axis=0)
        o_ref[0, :, h * VAL_DIM_B:(h + 1) * VAL_DIM_B] = _rms(o, gn).astype(o_ref.dtype)

    @pl.when(ti == pl.num_programs(1) - 1)
    def _():
        for h in heads:
            sfin_ref[0, h] = st[h].T


def _chunk_tri(t):
    r = np.arange(t)
    return jnp.asarray((r[:, None] // HGRN_CHUNK == r[None, :] // HGRN_CHUNK) & (r[None, :] <= r[:, None]), BF)


def _hgrn_prompt(layer, lb_logits, gnorm, qb, fb, ib, sel_flat, pt_flat, qkv_col, bias, r, state, cache_kt, cache_vt):
    b, s, _ = qb.shape
    t = 256
    nt = s // t
    ndb = r.shape[0]
    assert ndb == b * nt, "one sample sequence per hgrn_prompt grid step"
    _, h, dh, page = cache_kt.shape
    ntp = MOBA_TOPK * (MOBA_BLOCK // page)
    tok = pl.BlockSpec((1, t, WIDTH_B), lambda bi, ti, sl, pt: (bi, ti, 0))
    fixed = lambda a: pl.BlockSpec(a.shape, lambda bi, ti, sl, pt: (0,) * a.ndim)
    seq = lambda a: pl.BlockSpec((1,) + a.shape[1:], lambda bi, ti, sl, pt: (bi * nt + ti,) + (0,) * (a.ndim - 1))
    tri = _chunk_tri(t)
    r3 = r.reshape(ndb, 1, r.shape[1])
    oa_shape = jax.ShapeDtypeStruct((ndb, WIDTH_A, 1), F32)
    ob_shape = jax.ShapeDtypeStruct((ndb, 1, WIDTH_B), BF)
    sn_shape = jax.ShapeDtypeStruct(state.shape, F32)
    return pl.pallas_call(
        functools.partial(_hgrn_kernel, layer),
        grid_spec=pltpu.PrefetchScalarGridSpec(
            num_scalar_prefetch=2,
            grid=(b, nt),
            in_specs=[
                fixed(lb_logits), fixed(gnorm), fixed(tri), tok, tok, tok,
                seq(qkv_col), seq(bias), seq(r3), seq(state),
                pl.BlockSpec(memory_space=pl.ANY), pl.BlockSpec(memory_space=pl.ANY),
            ],
            out_specs=(
                tok,
                pl.BlockSpec((1, N_HEADS_B, KEY_DIM_B, VAL_DIM_B), lambda bi, ti, sl, pt: (bi, 0, 0, 0)),
                seq(oa_shape), seq(ob_shape), seq(sn_shape),
            ),
            scratch_shapes=[pltpu.VMEM((N_HEADS_B, VAL_DIM_B, KEY_DIM_B), F32),
                            pltpu.VMEM((2, ntp, h, dh, page), F32),
                            pltpu.VMEM((2, ntp, h, dh, page), F32),
                            pltpu.SemaphoreType.DMA((2, 2))],
        ),
        out_shape=(
            jax.ShapeDtypeStruct((b, s, WIDTH_B), BF),
            jax.ShapeDtypeStruct((b, N_HEADS_B, KEY_DIM_B, VAL_DIM_B), F32),
            oa_shape, ob_shape, sn_shape,
        ),
        compiler_params=_params("arbitrary", "arbitrary"),
        name="hgrn_prompt",
    )(sel_flat, pt_flat, lb_logits, gnorm, tri, qb, fb, ib, qkv_col, bias, r3, state, cache_kt, cache_vt)


def _out_kernel(x_ref, oa_ref, ob_ref, g_ref, w2_ref, wa_ref, wb_ref, wo_ref, gf_ref, y_ref):
    x = x_ref[...]
    hn = _rms(x, g_ref[...]).astype(BF)
    d = x.shape[1]

    def proj(lo, hi):
        return _dot(hn, w2_ref[:, lo:hi])

    ya = _dot((oa_ref[...].astype(F32) * _silu(proj(0, WIDTH_A))).astype(BF), wa_ref[...])
    yb = _dot((ob_ref[...].astype(F32) * _silu(proj(WIDTH_A, WIDTH_A + WIDTH_B))).astype(BF), wb_ref[...])
    g0 = WIDTH_A + WIDTH_B
    mixed = _sigmoid(proj(g0, g0 + d)) * ya + _sigmoid(proj(g0 + d, g0 + 2 * d)) * yb
    out = x + _dot(mixed.astype(BF), wo_ref[...])
    y_ref[...] = _rms(out, gf_ref[...])


def _out_proj(x2, oa, ob, g_in, w2, wa, wb, wo, g_fin, tm):
    n, d = x2.shape
    rows = lambda w: pl.BlockSpec((tm, w), lambda i: (i, 0))
    full = lambda a: pl.BlockSpec(a.shape, lambda i: (0, 0))
    return pl.pallas_call(
        _out_kernel,
        grid=(n // tm,),
        in_specs=[rows(d), rows(WIDTH_A), rows(WIDTH_B), full(g_in), full(w2), full(wa), full(wb), full(wo),
                  full(g_fin)],
        out_specs=rows(d),
        out_shape=jax.ShapeDtypeStruct((n, d), F32),
        compiler_params=_params("arbitrary"),
        name="out_proj",
    )(x2, oa, ob, g_in, w2, wa, wb, wo, g_fin)


def _proj_sample_kernel(x_ref, g_ref, w_ref, r_ref):
    hn = _rms(x_ref[...], g_ref[...]).astype(BF)
    for c in range(w_ref.shape[1] // WIDTH_A):
        cols = slice(c * WIDTH_A, (c + 1) * WIDTH_A)
        r_ref[:, cols] = _dot(hn, w_ref[:, cols])


def _proj_sample(x2, g_in, w1):
    n, _ = x2.shape
    return pl.pallas_call(
        _proj_sample_kernel,
        out_shape=jax.ShapeDtypeStruct((n, w1.shape[1]), F32),
        compiler_params=pltpu.CompilerParams(vmem_limit_bytes=VMEM_LIMIT),
        name="proj_sample",
    )(x2, g_in, w1)


def _sample_gate_kernel(past, slope_ref, r_ref, bs_ref, sel_ref, bias_ref):
    dpg = r_ref.shape[1]
    nblk = LANES // dpg
    page = bias_ref.shape[3]
    ppb = MOBA_BLOCK // page
    hrow = lax.broadcasted_iota(jnp.int32, (N_HEADS_A, WIDTH_A), 0)
    hcol = lax.broadcasted_iota(jnp.int32, (N_HEADS_A, WIDTH_A), 1) // HEAD_DIM_A
    lane = lax.broadcasted_iota(jnp.int32, (N_HEADS_A, LANES), 1)
    tok = lax.broadcasted_iota(jnp.int32, (N_HEADS_A, page), 1)
    kh, kl = _split2(bs_ref[0] / MOBA_BLOCK)
    for j in range(dpg):
        qbd = jnp.where(hrow == hcol, jnp.broadcast_to(r_ref[0, j:j + 1, :WIDTH_A], (N_HEADS_A, WIDTH_A)), 0.0)
        qh, ql = _split2(qbd)
        gate = _dot(qh, kh) + _dot(qh, kl) + _dot(ql, kh)
        blk = lane - j * nblk
        x = jnp.where((blk >= 0) & (blk < nblk), gate, -jnp.inf)
        out = jnp.zeros((N_HEADS_A, LANES), jnp.int32)
        for rank in range(MOBA_TOPK):
            m = jnp.max(x, axis=1, keepdims=True)
            first = jnp.min(jnp.where(x == m, blk, LANES), axis=1, keepdims=True)
            out = jnp.where(lane == rank, first, out)
            x = jnp.where(blk == first, -jnp.inf, x)
            for p in range(ppb):
                dist = past - (first * MOBA_BLOCK + p * page) - tok
                bias_ref[j, rank * ppb + p] = -slope_ref[...] * dist.astype(F32)
        for unused in range(MOBA_TOPK * ppb, bias_ref.shape[1]):
            bias_ref[j, unused] = jnp.zeros((N_HEADS_A, page), F32)
        sel_ref[j] = out


def _sample_gate(past, r, bsum_t, nblk, page):
    ndb = r.shape[0]
    dpg = LANES // nblk
    slots = pl.cdiv(MOBA_TOPK * (MOBA_BLOCK // page), 8) * 8
    slope_col = jnp.asarray(_alibi_slopes(N_HEADS_A), F32)[:, None]
    return pl.pallas_call(
        functools.partial(_sample_gate_kernel, past),
        grid=(ndb // dpg,),
        in_specs=[pl.BlockSpec((N_HEADS_A, 1), lambda i: (0, 0)),
                  pl.BlockSpec((1, dpg, r.shape[1]), lambda i: (i, 0, 0)),
                  pl.BlockSpec((1, WIDTH_A, LANES), lambda i: (i, 0, 0))],
        out_specs=(pl.BlockSpec((dpg, N_HEADS_A, LANES), lambda i: (i, 0, 0)),
                   pl.BlockSpec((dpg, slots, N_HEADS_A, page), lambda i: (i, 0, 0, 0))),
        out_shape=(jax.ShapeDtypeStruct((ndb, N_HEADS_A, LANES), jnp.int32),
                   jax.ShapeDtypeStruct((ndb, slots, N_HEADS_A, page), F32)),
        compiler_params=_params("arbitrary"),
        name="sample_gate",
    )(slope_col, r.reshape(ndb // dpg, dpg, r.shape[1]), bsum_t)


def _sample_attention(db, ndb, sel_ref, pt_ref, qkv_ref, bias_ref, ck_ref, cv_ref, o_ref, kbuf, vbuf, sem):
    npg = pt_ref.shape[0] // ndb
    ntp = kbuf.shape[1]
    ppb = ntp // MOBA_TOPK

    def copies(d, slot):
        out = []
        for h in range(N_HEADS_A):
            for tp in range(ntp):
                blk = sel_ref[(d * N_HEADS_A + h) * MOBA_TOPK + tp // ppb]
                pg = pt_ref[d * npg + blk * ppb + tp % ppb]
                out.append(pltpu.make_async_copy(ck_ref.at[pg, h], kbuf.at[slot, tp, h], sem.at[0, slot]))
                out.append(pltpu.make_async_copy(cv_ref.at[pg, h], vbuf.at[slot, tp, h], sem.at[1, slot]))
        return out

    slot = db % 2

    @pl.when(db == 0)
    def _():
        for cp in copies(0, 0):
            cp.start()

    @pl.when(db + 1 < ndb)
    def _():
        for cp in copies(db + 1, 1 - slot):
            cp.start()

    for cp in copies(db, slot):
        cp.wait()

    hd = (N_HEADS_A, HEAD_DIM_A, 1)
    qkv = qkv_ref[0]
    qc = (qkv[:WIDTH_A] * HEAD_DIM_A ** -0.5).reshape(hd)
    knc = qkv[WIDTH_A:2 * WIDTH_A].reshape(hd)
    vnc = qkv[2 * WIDTH_A:].reshape(hd)
    lg_new = jnp.sum(qc * knc, axis=1)
    lg = jnp.sum(kbuf[slot] * qc[None], axis=2) + bias_ref[0, :ntp]
    m = jnp.maximum(jnp.max(jnp.max(lg, axis=0), axis=1, keepdims=True), lg_new)
    p_new = jnp.exp(lg_new - m)
    p = jnp.exp(lg - m[None])
    l = jnp.sum(jnp.sum(p, axis=0), axis=1, keepdims=True) + p_new
    pv = jnp.sum(vbuf[slot] * p[:, :, None, :], axis=0)
    acc = jnp.sum(pv, axis=2, keepdims=True) + p_new[:, :, None] * vnc
    o_ref[0] = (acc / l[:, :, None]).reshape(WIDTH_A, 1)


def _hgrn_sample_step(layer, lbl_ref, gn_ref, r_ref, s_ref, o_ref, sn_ref):
    ndb = r_ref.shape[0]
    kk = KEY_DIM_B
    eye = (lax.broadcasted_iota(jnp.int32, (kk, kk), 0) == lax.broadcasted_iota(jnp.int32, (kk, kk), 1))
    ones = jnp.ones((kk, VAL_DIM_B), BF)

    def column(vec):
        diag = jnp.where(eye, jnp.broadcast_to(vec, (kk, kk)), 0.0)
        return sum(_dot(part, ones) for part in _split3(diag))

    lb_all = _lower_bound(lbl_ref[...], layer)
    q0, f0, i0 = WIDTH_A * 3, WIDTH_A * 3 + WIDTH_B, WIDTH_A * 3 + 2 * WIDTH_B
    for db in range(ndb):
        row = r_ref[db]
        outs = []
        for h in range(N_HEADS_B):
            cols = slice(h * kk, (h + 1) * kk)
            lb = lb_all[:, cols]
            q = row[:, q0 + h * kk:q0 + (h + 1) * kk]
            f = lb + (1.0 - lb) * _sigmoid(row[:, f0 + h * kk:f0 + (h + 1) * kk])
            v = row[:, i0 + h * VAL_DIM_B:i0 + (h + 1) * VAL_DIM_B]
            fcol = column(f)
            s_new = fcol * s_ref[db, h] + (1.0 - fcol) * v
            sn_ref[db, h] = s_new
            o = jnp.sum(column(q) * s_new, axis=0, keepdims=True)
            outs.append(_rms(o, gn_ref[:, h * VAL_DIM_B:(h + 1) * VAL_DIM_B]))
        o_ref[db] = jnp.concatenate(outs, axis=1).astype(o_ref.dtype)


def _q_extension():
    ext = np.zeros((N_HEADS_A, LANES), np.float32)
    for h, slope in enumerate(_alibi_slopes(N_HEADS_A)):
        rest = np.float32(slope * LOG2E)
        for part in range(SLOPE_PARTS):
            piece = np.float32(rest.astype(jnp.bfloat16))
            ext[h, SLOPE_LANE + part] = piece
            rest = np.float32(rest - piece)
    return jnp.asarray(ext)


def _block_onehot(nblk, t):
    lane = np.arange(MASK_LANES)[None, None, :]
    tgt = (np.arange(nblk)[None, :, None] * N_HEADS_A + np.arange(N_HEADS_A)[:, None, None])
    oh = (lane == tgt).astype(np.float32)
    return jnp.broadcast_to(jnp.asarray(oh, BF)[:, :, None, :], (N_HEADS_A, nblk, t, MASK_LANES))


def kernel(x_prompt, x_sample, cache_k, cache_v, state_hgrn, page_table, norm_in_g, w_in,
           hgrn_lb_logits, hgrn_norm_g, w_branch_a, w_branch_b, w_out, final_norm_g):
    b, s, d = x_prompt.shape
    ndb, ds, _ = x_sample.shape
    depth, n_pool, _, page, _ = cache_k.shape
    npg = page_table.shape[1]
    past = npg * page
    ppb = MOBA_BLOCK // page
    assert depth == 1 and ds == 1
    assert s % (MOBA_BLOCK * min(MOBA_GROUP)) == 0 and s // MOBA_BLOCK <= MAX_BLOCKS
    assert past % MOBA_BLOCK == 0 and past // MOBA_BLOCK >= MOBA_TOPK
    assert ppb == 2
    nblk_s = npg // ppb
    assert LANES % nblk_s == 0 and ndb % (LANES // nblk_s) == 0
    layer = 0

    wk = N_HEADS_B * KEY_DIM_B
    a3 = 3 * WIDTH_A
    wl = w_in[layer]
    w1 = jnp.concatenate([wl[:, :a3], wl[:, a3 + WIDTH_A:a3 + WIDTH_A + 2 * wk + WIDTH_B]], axis=1).astype(BF)
    zb0 = a3 + WIDTH_A + 2 * wk + WIDTH_B
    w2 = jnp.concatenate([wl[:, a3:a3 + WIDTH_A], wl[:, zb0:]], axis=1).astype(BF)
    wa = w_branch_a[layer].astype(BF)
    wb = w_branch_b[layer].astype(BF)
    wo = w_out[layer].astype(BF)
    g_in = norm_in_g[layer][None, :]
    g_fin = final_norm_g[None, :]
    gnorm = hgrn_norm_g[layer][None, :]
    slopes = jnp.asarray([m * LOG2E for m in _alibi_slopes(N_HEADS_A)], F32)

    pt_flat = page_table.reshape(ndb * npg)
    ckt = jnp.swapaxes(cache_k[layer], -1, -2)
    cvt = jnp.swapaxes(cache_v[layer], -1, -2)

    xs = x_sample.reshape(ndb, d)
    r = _proj_sample(xs, g_in, w1)
    qh, kh, mb, vt, k_pg, v_pg, qb, fb, ib = _proj_prompt(x_prompt, g_in, w1, _q_extension(), page)
    o_a, bsum_t = _moba_prompt(slopes, pt_flat, qh, mb, kh, vt, _block_onehot(s // MOBA_BLOCK, MOBA_BLOCK), ckt)
    sel, bias = _sample_gate(past, r, bsum_t, nblk_s, page)
    sel = sel[:, :, :MOBA_TOPK].reshape(ndb * N_HEADS_A * MOBA_TOPK)
    qkv_col = r[:, :3 * WIDTH_A].reshape(ndb, 3 * WIDTH_A, 1)
    o_b, s_prompt, oa_s, ob_s, s_sample = _hgrn_prompt(
        layer, hgrn_lb_logits, gnorm, qb, fb, ib, sel, pt_flat, qkv_col, bias, r, state_hgrn[layer], ckt, cvt)
    y_prompt = _out_proj(x_prompt.reshape(b * s, d), o_a.reshape(b * s, WIDTH_A), o_b.reshape(b * s, WIDTH_B),
                         g_in, w2, wa, wb, wo, g_fin, 1024).reshape(b, s, d)
    y_sample = _out_proj(xs, oa_s.reshape(ndb, WIDTH_A).astype(BF), ob_s.reshape(ndb, WIDTH_B),
                         g_in, w2, wa, wb, wo, g_fin, ndb).reshape(ndb, 1, d)

    k_s = r[:, WIDTH_A:2 * WIDTH_A].reshape(ndb, N_HEADS_A, 1, HEAD_DIM_A)
    v_s = r[:, 2 * WIDTH_A:3 * WIDTH_A].reshape(ndb, N_HEADS_A, 1, HEAD_DIM_A)
    k_p = jnp.swapaxes(k_pg, -1, -2)
    v_p = jnp.swapaxes(v_pg, -1, -2)
    return (y_prompt, y_sample, k_p[None], v_p[None], s_prompt[None], k_s[None], v_s[None], s_sample[None])
```

```python
import functools

import numpy as np
import jax
import jax.numpy as jnp
from jax import lax
from jax.experimental import pallas as pl
from jax.experimental.pallas import tpu as pltpu

N_HEADS_A = 8
HEAD_DIM_A = 64
WIDTH_A = N_HEADS_A * HEAD_DIM_A
MOBA_BLOCK = 256
MOBA_TOPK = 3
N_HEADS_B = 4
KEY_DIM_B = 128
VAL_DIM_B = 128
WIDTH_B = N_HEADS_B * VAL_DIM_B
HGRN_CHUNK = 32
NORM_EPS = 1e-6

LANES = 128
MASK_LANES = LANES
MAX_BLOCKS = MASK_LANES // N_HEADS_A
NEG = -(2.0 ** 30)
SLOPE_LANE = HEAD_DIM_A
SLOPE_PARTS = 3
LOG2E = 1.4426950408889634
BF16_ROWS = 16
V_ROWS = HEAD_DIM_A + BF16_ROWS
MOBA_HEADS = 4
MOBA_GROUP = (4, 2, 1)
VMEM_LIMIT = 48 * 1024 * 1024

F32 = jnp.float32
BF = jnp.bfloat16
NT = (((1,), (1,)), ((), ()))
TN = (((0,), (0,)), ((), ()))


def _alibi_slopes(n):
    return [2.0 ** (-8.0 * (i + 1) / n) for i in range(n)]


def _rms(x, g):
    ms = jnp.mean(x * x, axis=-1, keepdims=True)
    return x * lax.rsqrt(ms + NORM_EPS) * g


def _split2(a):
    hi = a.astype(BF)
    lo = (a - hi.astype(F32)).astype(BF)
    return hi, lo


def _split3(a):
    hi = a.astype(BF)
    r = a - hi.astype(F32)
    mid = r.astype(BF)
    lo = (r - mid.astype(F32)).astype(BF)
    return hi, mid, lo


def _dot(a, b):
    return jnp.dot(a, b, preferred_element_type=F32)


def _dot_nt(a, b):
    return lax.dot_general(a, b, NT, preferred_element_type=F32)


def _dot3_nt(a, b):
    ah, al = _split2(a)
    bh, bl = _split2(b)
    return _dot_nt(ah, bh) + _dot_nt(ah, bl) + _dot_nt(al, bh)


def _sigmoid(x):
    return 1.0 / (1.0 + jnp.exp(-x))


def _silu(x):
    return x * _sigmoid(x)


def _params(*sem):
    return pltpu.CompilerParams(dimension_semantics=sem, vmem_limit_bytes=VMEM_LIMIT)


def _topk_blocks(gate, valid):
    x = jnp.where(valid, gate, -jnp.inf)
    blk1 = lax.broadcasted_iota(jnp.int32, (1, MASK_LANES), 1) // N_HEADS_A
    rank = jnp.zeros(gate.shape, F32)
    for kb in range(1, MAX_BLOCKS):
        y = pltpu.roll(x, kb * N_HEADS_A, axis=1)
        lower_index = jnp.where(blk1 >= kb, 1.0, 0.0)
        rank = rank + jnp.where(y > x, 1.0, jnp.where(y == x, lower_index, 0.0))
    return jnp.where(valid, jnp.where(rank < MOBA_TOPK, 1.0, 0.0), 0.0)


def _proj_prompt_kernel(x_ref, g_ref, w_ref, qext_ref, qh_ref, kh_ref, mb_ref, vt_ref,
                        kpg_ref, vpg_ref, qb_ref, fb_ref, ib_ref, km_ref):
    i = pl.program_id(1)
    t = x_ref.shape[1]
    page = kpg_ref.shape[4]
    @pl.when(i == 0)
    def _():
        km_ref[...] = jnp.zeros(km_ref.shape, F32)

    hn = _rms(x_ref[0], g_ref[...]).astype(BF)

    def proj(c):
        return _dot(hn, w_ref[:, c * WIDTH_A:(c + 1) * WIDTH_A])

    lane = lax.broadcasted_iota(jnp.int32, (t, LANES), 1)
    row = lax.broadcasted_iota(jnp.int32, (t, LANES), 0)
    q = proj(0)
    gate = _dot3_nt(q, km_ref[...])
    blk = lane // N_HEADS_A
    sel = _topk_blocks(gate, blk < i)
    mb_ref[0] = jnp.where(sel > 0.0, 0.0, NEG).astype(BF)

    k = proj(1)
    kmean = jnp.sum(k, axis=0, keepdims=True) / t
    hrow = lax.broadcasted_iota(jnp.int32, (N_HEADS_A, WIDTH_A), 0)
    hcol = lax.broadcasted_iota(jnp.int32, (N_HEADS_A, WIDTH_A), 1) // HEAD_DIM_A
    km_ref[pl.ds(pl.multiple_of(i * N_HEADS_A, N_HEADS_A), N_HEADS_A), :] = jnp.where(
        hrow == hcol, jnp.broadcast_to(kmean, (N_HEADS_A, WIDTH_A)), 0.0)

    low = lane < HEAD_DIM_A
    kext = jnp.where((lane >= SLOPE_LANE) & (lane < SLOPE_LANE + SLOPE_PARTS), row.astype(F32), 0.0)
    scale = HEAD_DIM_A ** -0.5 * LOG2E
    for hp in range(N_HEADS_A // 2):
        cols = slice(hp * LANES, (hp + 1) * LANES)
        qp = q[:, cols] * scale
        kp = k[:, cols]
        qpr = pltpu.roll(qp, HEAD_DIM_A, axis=1)
        kpr = pltpu.roll(kp, HEAD_DIM_A, axis=1)
        qh_ref[0, 2 * hp] = jnp.where(low, qp, qext_ref[2 * hp:2 * hp + 1, :]).astype(BF)
        qh_ref[0, 2 * hp + 1] = jnp.where(low, qpr, qext_ref[2 * hp + 1:2 * hp + 2, :]).astype(BF)
        kh_ref[0, 2 * hp, 0] = jnp.where(low, kp, kext).astype(BF)
        kh_ref[0, 2 * hp + 1, 0] = jnp.where(low, kpr, kext).astype(BF)

    v = proj(2)
    kt = k.T
    vt = v.T
    ones_row = (lax.broadcasted_iota(jnp.int32, (BF16_ROWS, t), 0) == 0).astype(F32)
    for h in range(N_HEADS_A):
        feat = slice(h * HEAD_DIM_A, (h + 1) * HEAD_DIM_A)
        vt_ref[0, h, 0] = jnp.concatenate([vt[feat, :], ones_row], axis=0).astype(BF)
        for p in range(t // page):
            toks = slice(p * page, (p + 1) * page)
            kpg_ref[0, p, h] = kt[feat, toks]
            vpg_ref[0, p, h] = vt[feat, toks]

    qb_ref[0] = proj(3)
    fb_ref[0] = proj(4)
    ib_ref[0] = proj(5)


def _proj_prompt(x, g_in, w1, qext, page):
    b, s, d = x.shape
    t = MOBA_BLOCK
    nblk = s // t
    nproj = w1.shape[1]
    out_shape = (
        jax.ShapeDtypeStruct((b, N_HEADS_A, s, LANES), BF),
        jax.ShapeDtypeStruct((b, N_HEADS_A, nblk, t, LANES), BF),
        jax.ShapeDtypeStruct((b, s, MASK_LANES), BF),
        jax.ShapeDtypeStruct((b, N_HEADS_A, nblk, V_ROWS, t), BF),
        jax.ShapeDtypeStruct((b, s // page, N_HEADS_A, HEAD_DIM_A, page), F32),
        jax.ShapeDtypeStruct((b, s // page, N_HEADS_A, HEAD_DIM_A, page), F32),
        jax.ShapeDtypeStruct((b, s, WIDTH_B), F32),
        jax.ShapeDtypeStruct((b, s, WIDTH_B), F32),
        jax.ShapeDtypeStruct((b, s, WIDTH_B), F32),
    )
    pg = pl.BlockSpec((1, t // page, N_HEADS_A, HEAD_DIM_A, page), lambda bi, i: (bi, i, 0, 0, 0))
    row3 = lambda w: pl.BlockSpec((1, t, w), lambda bi, i: (bi, i, 0))
    return pl.pallas_call(
        _proj_prompt_kernel,
        grid=(b, nblk),
        in_specs=[
            row3(d),
            pl.BlockSpec((1, d), lambda bi, i: (0, 0)),
            pl.BlockSpec((d, nproj), lambda bi, i: (0, 0)),
            pl.BlockSpec((N_HEADS_A, LANES), lambda bi, i: (0, 0)),
        ],
        out_specs=(
            pl.BlockSpec((1, N_HEADS_A, t, LANES), lambda bi, i: (bi, 0, i, 0)),
            pl.BlockSpec((1, N_HEADS_A, 1, t, LANES), lambda bi, i: (bi, 0, i, 0, 0)),
            row3(MASK_LANES),
            pl.BlockSpec((1, N_HEADS_A, 1, V_ROWS, t), lambda bi, i: (bi, 0, i, 0, 0)),
            pg, pg, row3(WIDTH_B), row3(WIDTH_B), row3(WIDTH_B),
        ),
        out_shape=out_shape,
        scratch_shapes=[pltpu.VMEM((MASK_LANES, WIDTH_A), F32)],
        compiler_params=_params("arbitrary", "arbitrary"),
        name="proj_prompt",
    )(x, g_in, w1, qext)


def _moba_kernel(slopes_ref, pt_ref, qh_ref, mb_ref, kh_ref, vt_ref, oh_ref, ck_ref, o_ref, bs_ref, pbuf, psem):
    hg = pl.program_id(1)
    qi = pl.program_id(2)
    step = (pl.program_id(0) * pl.num_programs(1) + hg) * pl.num_programs(2) + qi
    nsteps = pl.num_programs(0) * pl.num_programs(1) * pl.num_programs(2)
    _page_sums(step, nsteps, pt_ref, ck_ref, bs_ref, pbuf, psem)

    t = mb_ref.shape[1]
    mb = mb_ref[0]
    krow = lax.broadcasted_iota(jnp.int32, (t, t), 0)
    qcol = lax.broadcasted_iota(jnp.int32, (t, t), 1)
    causal = krow <= qcol
    heads = range(qh_ref.shape[1])
    q_aug = [jnp.concatenate([qh_ref[0, hh], mb], axis=1) for hh in heads]
    slope = [slopes_ref[hg * len(heads) + hh] for hh in heads]
    no_mask = jnp.zeros((t, MASK_LANES), BF)

    s_own = [_dot_nt(jnp.concatenate([kh_ref[0, hh, qi], no_mask], axis=1), q_aug[hh]) for hh in heads]
    carry = []
    for hh in heads:
        s = jnp.where(causal, s_own[hh], NEG)
        m = jnp.max(s, axis=0, keepdims=True)
        carry.append((m, jnp.exp2(s - m).astype(BF)))
    carry = tuple((m, _dot(vt_ref[0, hh, qi], p)) for hh, (m, p) in zip(heads, carry))

    def past_group(grp, first, g, carry):
        j0 = first + g * grp
        s_all = []
        for hh in heads:
            k_aug = jnp.concatenate([kh_ref[0, hh, pl.ds(j0, grp)].reshape(grp * t, LANES),
                                     oh_ref[hh, pl.ds(j0, grp)].reshape(grp * t, MASK_LANES)], axis=1)
            s_all.append(_dot_nt(k_aug, q_aug[hh]))
        stats = []
        for hh in heads:
            m, _ = carry[hh]
            subs = [s_all[hh][i * t:(i + 1) * t] for i in range(grp)]
            cj = [slope[hh] * ((j0 + i - qi) * t).astype(F32) for i in range(grp)]
            m_new = m
            for i in range(grp):
                m_new = jnp.maximum(m_new, jnp.max(subs[i], axis=0, keepdims=True) + cj[i])
            p = jnp.concatenate([jnp.exp2(subs[i] - (m_new - cj[i])).astype(BF) for i in range(grp)], axis=0)
            stats.append((m_new, jnp.exp2(m - m_new), p))
        out = []
        for hh in heads:
            m_new, alpha, p = stats[hh]
            vt = jnp.concatenate([vt_ref[0, hh, j0 + i] for i in range(grp)], axis=1)
            out.append((m_new, alpha * carry[hh][1] + _dot(vt, p)))
        return tuple(out)

    first = 0
    for grp in MOBA_GROUP:
        n = (qi - first) // grp
        carry = lax.fori_loop(0, n, functools.partial(past_group, grp, first), carry)
        first = first + n * grp
    o_ref[0] = jnp.concatenate(
        [(acc[:HEAD_DIM_A] / acc[HEAD_DIM_A:HEAD_DIM_A + 1]).T for (_, acc) in carry], axis=1).astype(o_ref.dtype)


def _page_sums(step, nsteps, pt_ref, ck_ref, bs_ref, pbuf, psem):
    pps = pbuf.shape[1]
    half = pps // 2

    def copy(s, slot, i):
        page = pt_ref[s * pps + (2 * i if i < half else 2 * (i - half) + 1)]
        return pltpu.make_async_copy(ck_ref.at[page], pbuf.at[slot, i], psem.at[slot])

    slot = step % 2

    @pl.when(step == 0)
    def _():
        for i in range(pps):
            copy(0, 0, i).start()

    @pl.when(step + 1 < nsteps)
    def _():
        for i in range(pps):
            copy(step + 1, 1 - slot, i).start()

    base = (step * half) % LANES

    @pl.when(base == 0)
    def _():
        bs_ref[...] = jnp.zeros(bs_ref.shape, F32)

    for i in range(pps):
        copy(step, slot, i).wait()
    lane = lax.broadcasted_iota(jnp.int32, (HEAD_DIM_A, LANES), 1)
    for h in range(N_HEADS_A):
        feat = slice(h * HEAD_DIM_A, (h + 1) * HEAD_DIM_A)
        tile = bs_ref[0, feat, :]
        for i in range(half):
            red = jnp.sum(pbuf[slot, i, h] + pbuf[slot, half + i, h], axis=1, keepdims=True)
            tile = jnp.where(lane == base + i, red, tile)
        bs_ref[0, feat, :] = tile


def _moba_prompt(slopes, pt_flat, qh, mb, kh, vt, onehot, cache_kt):
    b, _, s, _ = qh.shape
    t = MOBA_BLOCK
    nblk = s // t
    hg = MOBA_HEADS
    vrows = vt.shape[3]
    _, h, dh, page = cache_kt.shape
    ngrp = N_HEADS_A // hg
    nsteps = b * ngrp * nblk
    pps = pt_flat.shape[0] // nsteps
    half = pps // 2
    assert pps * nsteps == pt_flat.shape[0] and pps % 2 == 0 and LANES % half == 0
    bs_shape = (pl.cdiv(nsteps * half, LANES), WIDTH_A, LANES)
    return pl.pallas_call(
        _moba_kernel,
        grid_spec=pltpu.PrefetchScalarGridSpec(
            num_scalar_prefetch=2,
            grid=(b, ngrp, nblk),
            in_specs=[
                pl.BlockSpec((1, hg, t, LANES), lambda bi, g, qi, sl, pt: (bi, g, qi, 0)),
                pl.BlockSpec((1, t, MASK_LANES), lambda bi, g, qi, sl, pt: (bi, qi, 0)),
                pl.BlockSpec((1, hg, nblk, t, LANES), lambda bi, g, qi, sl, pt: (bi, g, 0, 0, 0)),
                pl.BlockSpec((1, hg, nblk, vrows, t), lambda bi, g, qi, sl, pt: (bi, g, 0, 0, 0)),
                pl.BlockSpec((hg, nblk, t, MASK_LANES), lambda bi, g, qi, sl, pt: (g, 0, 0, 0)),
                pl.BlockSpec(memory_space=pl.ANY),
            ],
            out_specs=(
                pl.BlockSpec((1, t, hg * HEAD_DIM_A), lambda bi, g, qi, sl, pt: (bi, qi, g)),
                pl.BlockSpec((1, WIDTH_A, LANES),
                             lambda bi, g, qi, sl, pt: ((((bi * ngrp + g) * nblk + qi) * half) // LANES, 0, 0)),
            ),
            scratch_shapes=[pltpu.VMEM((2, pps, h, dh, page), F32), pltpu.SemaphoreType.DMA((2,))],
        ),
        out_shape=(jax.ShapeDtypeStruct((b, s, WIDTH_A), BF), jax.ShapeDtypeStruct(bs_shape, F32)),
        compiler_params=_params("arbitrary", "arbitrary", "arbitrary"),
        name="moba_prompt",
    )(slopes, pt_flat, qh, mb, kh, vt, onehot, cache_kt)


def _lower_bound(lb_logits, layer):
    e = jnp.exp(lb_logits - jnp.max(lb_logits, axis=0, keepdims=True))
    return jnp.sum(e[:layer + 1], axis=0, keepdims=True) / jnp.sum(e, axis=0, keepdims=True)


def _hgrn_kernel(layer, sel_ref, pt_ref, lbl_ref, gn_ref, tri_ref, q_ref, f_ref, i_ref,
                 qkv_ref, bias_ref, r_ref, s0_ref, ck_ref, cv_ref,
                 o_ref, sfin_ref, oa_ref, ob_ref, sn_ref, st_ref, kbuf, vbuf, sem):
    ti = pl.program_id(1)
    step = pl.program_id(0) * pl.num_programs(1) + ti
    nsteps = pl.num_programs(0) * pl.num_programs(1)
    _sample_attention(step, nsteps, sel_ref, pt_ref, qkv_ref, bias_ref, ck_ref, cv_ref, oa_ref, kbuf, vbuf, sem)
    _hgrn_sample_step(layer, lbl_ref, gn_ref, r_ref, s0_ref, ob_ref, sn_ref)

    t = q_ref.shape[1]
    c = HGRN_CHUNK
    heads = range(N_HEADS_B)
    hk = lambda a, h: a[:, h * KEY_DIM_B:(h + 1) * KEY_DIM_B]

    @pl.when(ti == 0)
    def _():
        st_ref[...] = jnp.zeros(st_ref.shape, F32)

    lb = _lower_bound(lbl_ref[...], layer)
    f = lb + (1.0 - lb) * _sigmoid(f_ref[0])
    k = 1.0 - f
    tri = tri_ref[...]
    cum = sum(_dot(tri, part) for part in _split3(jnp.log(f)))
    tot = jnp.concatenate(
        [jnp.broadcast_to(cum[(n + 1) * c - 1:(n + 1) * c], (c, cum.shape[1])) for n in range(t // c)], axis=0)
    qd = (q_ref[0] * jnp.exp(cum)).astype(BF)
    ki = (k * jnp.exp(-cum)).astype(BF)
    kd = (k * jnp.exp(tot - cum)).astype(BF)
    vb = i_ref[0].astype(BF)
    a = [jnp.where(tri > 0, _dot_nt(hk(qd, h), hk(ki, h)), 0.0).astype(BF) for h in heads]
    o_intra = [_dot(a[h], hk(vb, h)) for h in heads]
    chunks = [slice(n * c, (n + 1) * c) for n in range(t // c)]
    upd = [[lax.dot_general(hk(vb, h)[r], hk(kd, h)[r], TN, preferred_element_type=F32) for h in heads]
           for r in chunks]
    st = [st_ref[h] for h in heads]
    st_in = []
    for n, r in enumerate(chunks):
        st_in.append([st[h].astype(BF) for h in heads])
        decay = jnp.exp(tot[r.stop - 1:r.stop])
        st = [st[h] * hk(decay, h) + upd[n][h] for h in heads]
    inter = [[_dot_nt(hk(qd, h)[r], st_in[n][h]) for h in heads] for n, r in enumerate(chunks)]
    for h in heads:
        st_ref[h] = st[h]
        gn = gn_ref[:, h * VAL_DIM_B:(h + 1) * VAL_DIM_B]
        o = o_intra[h] + jnp.concatenate([inter[n][h] for n in range(len(chunks))], axis=0)
        o_ref[0, :, h * VAL_DIM_B:(h + 1) * VAL_DIM_B] = _rms(o, gn).astype(o_ref.dtype)

    @pl.when(ti == pl.num_programs(1) - 1)
    def _():
        for h in heads:
            sfin_ref[0, h] = st[h].T


def _chunk_tri(t):
    r = np.arange(t)
    return jnp.asarray((r[:, None] // HGRN_CHUNK == r[None, :] // HGRN_CHUNK) & (r[None, :] <= r[:, None]), BF)


def _hgrn_prompt(layer, lb_logits, gnorm, qb, fb, ib, sel_flat, pt_flat, qkv_rows, bias, r, state, cache_kt, cache_vt):
    b, s, _ = qb.shape
    t = 256
    nt = s // t
    ndb = r.shape[0]
    assert ndb == b * nt, "one sample sequence per hgrn_prompt grid step"
    _, h, dh, page = cache_kt.shape
    ntp = MOBA_TOPK * (MOBA_BLOCK // page)
    tok = pl.BlockSpec((1, t, WIDTH_B), lambda bi, ti, sl, pt: (bi, ti, 0))
    fixed = lambda a: pl.BlockSpec(a.shape, lambda bi, ti, sl, pt: (0,) * a.ndim)
    seq = lambda a: pl.BlockSpec((1,) + a.shape[1:], lambda bi, ti, sl, pt: (bi * nt + ti,) + (0,) * (a.ndim - 1))
    tri = _chunk_tri(t)
    r3 = r.reshape(ndb, 1, r.shape[1])
    oa_shape = jax.ShapeDtypeStruct((ndb, 8, LANES), F32)
    ob_shape = jax.ShapeDtypeStruct((ndb, 1, WIDTH_B), BF)
    sn_shape = jax.ShapeDtypeStruct(state.shape, F32)
    return pl.pallas_call(
        functools.partial(_hgrn_kernel, layer),
        grid_spec=pltpu.PrefetchScalarGridSpec(
            num_scalar_prefetch=2,
            grid=(b, nt),
            in_specs=[
                fixed(lb_logits), fixed(gnorm), fixed(tri), tok, tok, tok,
                seq(qkv_rows), seq(bias), seq(r3), seq(state),
                pl.BlockSpec(memory_space=pl.ANY), pl.BlockSpec(memory_space=pl.ANY),
            ],
            out_specs=(
                tok,
                pl.BlockSpec((1, N_HEADS_B, KEY_DIM_B, VAL_DIM_B), lambda bi, ti, sl, pt: (bi, 0, 0, 0)),
                seq(oa_shape), seq(ob_shape), seq(sn_shape),
            ),
            scratch_shapes=[pltpu.VMEM((N_HEADS_B, VAL_DIM_B, KEY_DIM_B), F32),
                            pltpu.VMEM((2, ntp, h, dh, page), F32),
                            pltpu.VMEM((2, ntp, h, dh, page), F32),
                            pltpu.SemaphoreType.DMA((2, 2))],
        ),
        out_shape=(
            jax.ShapeDtypeStruct((b, s, WIDTH_B), BF),
            jax.ShapeDtypeStruct((b, N_HEADS_B, KEY_DIM_B, VAL_DIM_B), F32),
            oa_shape, ob_shape, sn_shape,
        ),
        compiler_params=_params("arbitrary", "arbitrary"),
        name="hgrn_prompt",
    )(sel_flat, pt_flat, lb_logits, gnorm, tri, qb, fb, ib, qkv_rows, bias, r3, state, cache_kt, cache_vt)


def _out_kernel(x_ref, oa_ref, ob_ref, g_ref, w2_ref, wa_ref, wb_ref, wo_ref, gf_ref, y_ref):
    x = x_ref[...]
    hn = _rms(x, g_ref[...]).astype(BF)
    d = x.shape[1]

    def proj(lo, hi):
        return _dot(hn, w2_ref[:, lo:hi])

    ya = _dot((oa_ref[...].astype(F32) * _silu(proj(0, WIDTH_A))).astype(BF), wa_ref[...])
    yb = _dot((ob_ref[...].astype(F32) * _silu(proj(WIDTH_A, WIDTH_A + WIDTH_B))).astype(BF), wb_ref[...])
    g0 = WIDTH_A + WIDTH_B
    mixed = _sigmoid(proj(g0, g0 + d)) * ya + _sigmoid(proj(g0 + d, g0 + 2 * d)) * yb
    out = x + _dot(mixed.astype(BF), wo_ref[...])
    y_ref[...] = _rms(out, gf_ref[...])


def _out_proj(x2, oa, ob, g_in, w2, wa, wb, wo, g_fin, tm):
    n, d = x2.shape
    rows = lambda w: pl.BlockSpec((tm, w), lambda i: (i, 0))
    full = lambda a: pl.BlockSpec(a.shape, lambda i: (0, 0))
    return pl.pallas_call(
        _out_kernel,
        grid=(n // tm,),
        in_specs=[rows(d), rows(WIDTH_A), rows(WIDTH_B), full(g_in), full(w2), full(wa), full(wb), full(wo),
                  full(g_fin)],
        out_specs=rows(d),
        out_shape=jax.ShapeDtypeStruct((n, d), F32),
        compiler_params=_params("arbitrary"),
        name="out_proj",
    )(x2, oa, ob, g_in, w2, wa, wb, wo, g_fin)


def _proj_sample_kernel(x_ref, g_ref, w_ref, r_ref):
    hn = _rms(x_ref[...], g_ref[...]).astype(BF)
    for c in range(w_ref.shape[1] // WIDTH_A):
        cols = slice(c * WIDTH_A, (c + 1) * WIDTH_A)
        r_ref[:, cols] = _dot(hn, w_ref[:, cols])


def _proj_sample(x2, g_in, w1):
    n, _ = x2.shape
    return pl.pallas_call(
        _proj_sample_kernel,
        out_shape=jax.ShapeDtypeStruct((n, w1.shape[1]), F32),
        compiler_params=pltpu.CompilerParams(vmem_limit_bytes=VMEM_LIMIT),
        name="proj_sample",
    )(x2, g_in, w1)


def _sample_gate_kernel(past, slope_ref, r_ref, bs_ref, sel_ref, bias_ref):
    dpg = r_ref.shape[1]
    nblk = LANES // dpg
    page = bias_ref.shape[3]
    ppb = MOBA_BLOCK // page
    hrow = lax.broadcasted_iota(jnp.int32, (N_HEADS_A, WIDTH_A), 0)
    hcol = lax.broadcasted_iota(jnp.int32, (N_HEADS_A, WIDTH_A), 1) // HEAD_DIM_A
    lane = lax.broadcasted_iota(jnp.int32, (N_HEADS_A, LANES), 1)
    tok = lax.broadcasted_iota(jnp.int32, (N_HEADS_A, page), 1)
    kh, kl = _split2(bs_ref[0] / MOBA_BLOCK)
    for j in range(dpg):
        qbd = jnp.where(hrow == hcol, jnp.broadcast_to(r_ref[0, j:j + 1, :WIDTH_A], (N_HEADS_A, WIDTH_A)), 0.0)
        qh, ql = _split2(qbd)
        gate = _dot(qh, kh) + _dot(qh, kl) + _dot(ql, kh)
        blk = lane - j * nblk
        x = jnp.where((blk >= 0) & (blk < nblk), gate, -jnp.inf)
        out = jnp.zeros((N_HEADS_A, LANES), jnp.int32)
        for rank in range(MOBA_TOPK):
            m = jnp.max(x, axis=1, keepdims=True)
            first = jnp.min(jnp.where(x == m, blk, LANES), axis=1, keepdims=True)
            out = jnp.where(lane == rank, first, out)
            x = jnp.where(blk == first, -jnp.inf, x)
            for p in range(ppb):
                dist = past - (first * MOBA_BLOCK + p * page) - tok
                bias_ref[j, rank * ppb + p] = -slope_ref[...] * dist.astype(F32)
        for unused in range(MOBA_TOPK * ppb, bias_ref.shape[1]):
            bias_ref[j, unused] = jnp.zeros((N_HEADS_A, page), F32)
        sel_ref[j] = out


def _sample_gate(past, r, bsum_t, nblk, page):
    ndb = r.shape[0]
    dpg = LANES // nblk
    slots = pl.cdiv(MOBA_TOPK * (MOBA_BLOCK // page), 8) * 8
    slope_col = jnp.asarray(_alibi_slopes(N_HEADS_A), F32)[:, None]
    return pl.pallas_call(
        functools.partial(_sample_gate_kernel, past),
        grid=(ndb // dpg,),
        in_specs=[pl.BlockSpec((N_HEADS_A, 1), lambda i: (0, 0)),
                  pl.BlockSpec((1, dpg, r.shape[1]), lambda i: (i, 0, 0)),
                  pl.BlockSpec((1, WIDTH_A, LANES), lambda i: (i, 0, 0))],
        out_specs=(pl.BlockSpec((dpg, N_HEADS_A, LANES), lambda i: (i, 0, 0)),
                   pl.BlockSpec((dpg, slots, N_HEADS_A, page), lambda i: (i, 0, 0, 0))),
        out_shape=(jax.ShapeDtypeStruct((ndb, N_HEADS_A, LANES), jnp.int32),
                   jax.ShapeDtypeStruct((ndb, slots, N_HEADS_A, page), F32)),
        compiler_params=_params("arbitrary"),
        name="sample_gate",
    )(slope_col, r.reshape(ndb // dpg, dpg, r.shape[1]), bsum_t)


def _sample_attention(db, ndb, sel_ref, pt_ref, qkv_ref, bias_ref, ck_ref, cv_ref, o_ref, kbuf, vbuf, sem):
    npg = pt_ref.shape[0] // ndb
    ntp = kbuf.shape[1]
    ppb = ntp // MOBA_TOPK

    def copies(d, slot):
        out = []
        for h in range(N_HEADS_A):
            for tp in range(ntp):
                blk = sel_ref[(d * N_HEADS_A + h) * MOBA_TOPK + tp // ppb]
                pg = pt_ref[d * npg + blk * ppb + tp % ppb]
                out.append(pltpu.make_async_copy(ck_ref.at[pg, h], kbuf.at[slot, tp, h], sem.at[0, slot]))
                out.append(pltpu.make_async_copy(cv_ref.at[pg, h], vbuf.at[slot, tp, h], sem.at[1, slot]))
        return out

    slot = db % 2

    @pl.when(db == 0)
    def _():
        for cp in copies(0, 0):
            cp.start()

    @pl.when(db + 1 < ndb)
    def _():
        for cp in copies(db + 1, 1 - slot):
            cp.start()

    for cp in copies(db, slot):
        cp.wait()

    rows = qkv_ref.shape[1]
    xt = jnp.concatenate([qkv_ref[0], jnp.zeros((LANES - rows, LANES), F32)], axis=0).T
    hpc = LANES // HEAD_DIM_A

    def head_columns(first_col):
        return jnp.stack([xt[(h % hpc) * HEAD_DIM_A:(h % hpc + 1) * HEAD_DIM_A,
                             first_col + h // hpc:first_col + h // hpc + 1] for h in range(N_HEADS_A)])

    ncol = WIDTH_A // LANES
    qc = head_columns(0) * HEAD_DIM_A ** -0.5
    knc = head_columns(ncol)
    vnc = head_columns(2 * ncol)
    lg_new = jnp.sum(qc * knc, axis=1)
    lg = jnp.sum(kbuf[slot] * qc[None], axis=2) + bias_ref[0, :ntp]
    m = jnp.maximum(jnp.max(jnp.max(lg, axis=0), axis=1, keepdims=True), lg_new)
    p_new = jnp.exp(lg_new - m)
    p = jnp.exp(lg - m[None])
    l = jnp.sum(jnp.sum(p, axis=0), axis=1, keepdims=True) + p_new
    pv = jnp.sum(vbuf[slot] * p[:, :, None, :], axis=0)
    acc = jnp.sum(pv, axis=2, keepdims=True) + p_new[:, :, None] * vnc
    o = (acc / l[:, :, None]).reshape(ncol, LANES, 1)
    lane = lax.broadcasted_iota(jnp.int32, (LANES, LANES), 1)
    cols = jnp.zeros((LANES, LANES), F32)
    for c in range(ncol):
        cols = jnp.where(lane == c, jnp.broadcast_to(o[c], (LANES, LANES)), cols)
    o_ref[0] = cols.T[:o_ref.shape[1]]


def _hgrn_sample_step(layer, lbl_ref, gn_ref, r_ref, s_ref, o_ref, sn_ref):
    ndb = r_ref.shape[0]
    kk = KEY_DIM_B
    eye = (lax.broadcasted_iota(jnp.int32, (kk, kk), 0) == lax.broadcasted_iota(jnp.int32, (kk, kk), 1))
    ones = jnp.ones((kk, VAL_DIM_B), BF)

    def column(vec):
        diag = jnp.where(eye, jnp.broadcast_to(vec, (kk, kk)), 0.0)
        return sum(_dot(part, ones) for part in _split3(diag))

    lb_all = _lower_bound(lbl_ref[...], layer)
    q0, f0, i0 = WIDTH_A * 3, WIDTH_A * 3 + WIDTH_B, WIDTH_A * 3 + 2 * WIDTH_B
    for db in range(ndb):
        row = r_ref[db]
        outs = []
        for h in range(N_HEADS_B):
            cols = slice(h * kk, (h + 1) * kk)
            lb = lb_all[:, cols]
            q = row[:, q0 + h * kk:q0 + (h + 1) * kk]
            f = lb + (1.0 - lb) * _sigmoid(row[:, f0 + h * kk:f0 + (h + 1) * kk])
            v = row[:, i0 + h * VAL_DIM_B:i0 + (h + 1) * VAL_DIM_B]
            fcol = column(f)
            s_new = fcol * s_ref[db, h] + (1.0 - fcol) * v
            sn_ref[db, h] = s_new
            o = jnp.sum(column(q) * s_new, axis=0, keepdims=True)
            outs.append(_rms(o, gn_ref[:, h * VAL_DIM_B:(h + 1) * VAL_DIM_B]))
        o_ref[db] = jnp.concatenate(outs, axis=1).astype(o_ref.dtype)


def _q_extension():
    ext = np.zeros((N_HEADS_A, LANES), np.float32)
    for h, slope in enumerate(_alibi_slopes(N_HEADS_A)):
        rest = np.float32(slope * LOG2E)
        for part in range(SLOPE_PARTS):
            piece = np.float32(rest.astype(jnp.bfloat16))
            ext[h, SLOPE_LANE + part] = piece
            rest = np.float32(rest - piece)
    return jnp.asarray(ext)


def _block_onehot(nblk, t):
    lane = np.arange(MASK_LANES)[None, None, :]
    tgt = (np.arange(nblk)[None, :, None] * N_HEADS_A + np.arange(N_HEADS_A)[:, None, None])
    oh = (lane == tgt).astype(np.float32)
    return jnp.broadcast_to(jnp.asarray(oh, BF)[:, :, None, :], (N_HEADS_A, nblk, t, MASK_LANES))


def kernel(x_prompt, x_sample, cache_k, cache_v, state_hgrn, page_table, norm_in_g, w_in,
           hgrn_lb_logits, hgrn_norm_g, w_branch_a, w_branch_b, w_out, final_norm_g):
    b, s, d = x_prompt.shape
    ndb, ds, _ = x_sample.shape
    depth, n_pool, _, page, _ = cache_k.shape
    npg = page_table.shape[1]
    past = npg * page
    ppb = MOBA_BLOCK // page
    assert depth == 1 and ds == 1
    assert s % (MOBA_BLOCK * min(MOBA_GROUP)) == 0 and s // MOBA_BLOCK <= MAX_BLOCKS
    assert past % MOBA_BLOCK == 0 and past // MOBA_BLOCK >= MOBA_TOPK
    assert ppb == 2
    nblk_s = npg // ppb
    assert LANES % nblk_s == 0 and ndb % (LANES // nblk_s) == 0
    layer = 0

    wk = N_HEADS_B * KEY_DIM_B
    a3 = 3 * WIDTH_A
    wl = w_in[layer]
    w1 = jnp.concatenate([wl[:, :a3], wl[:, a3 + WIDTH_A:a3 + WIDTH_A + 2 * wk + WIDTH_B]], axis=1).astype(BF)
    zb0 = a3 + WIDTH_A + 2 * wk + WIDTH_B
    w2 = jnp.concatenate([wl[:, a3:a3 + WIDTH_A], wl[:, zb0:]], axis=1).astype(BF)
    wa = w_branch_a[layer].astype(BF)
    wb = w_branch_b[layer].astype(BF)
    wo = w_out[layer].astype(BF)
    g_in = norm_in_g[layer][None, :]
    g_fin = final_norm_g[None, :]
    gnorm = hgrn_norm_g[layer][None, :]
    slopes = jnp.asarray([m * LOG2E for m in _alibi_slopes(N_HEADS_A)], F32)

    pt_flat = page_table.reshape(ndb * npg)
    ckt = jnp.swapaxes(cache_k[layer], -1, -2)
    cvt = jnp.swapaxes(cache_v[layer], -1, -2)

    xs = x_sample.reshape(ndb, d)
    r = _proj_sample(xs, g_in, w1)
    qh, kh, mb, vt, k_pg, v_pg, qb, fb, ib = _proj_prompt(x_prompt, g_in, w1, _q_extension(), page)
    o_a, bsum_t = _moba_prompt(slopes, pt_flat, qh, mb, kh, vt, _block_onehot(s // MOBA_BLOCK, MOBA_BLOCK), ckt)
    sel, bias = _sample_gate(past, r, bsum_t, nblk_s, page)
    sel = sel[:, :, :MOBA_TOPK].reshape(ndb * N_HEADS_A * MOBA_TOPK)
    qkv_rows = r[:, :3 * WIDTH_A].reshape(ndb, 3 * WIDTH_A // LANES, LANES)
    o_b, s_prompt, oa_s, ob_s, s_sample = _hgrn_prompt(
        layer, hgrn_lb_logits, gnorm, qb, fb, ib, sel, pt_flat, qkv_rows, bias, r, state_hgrn[layer], ckt, cvt)
    oa_s = oa_s[:, :WIDTH_A // LANES]
    y_prompt = _out_proj(x_prompt.reshape(b * s, d), o_a.reshape(b * s, WIDTH_A), o_b.reshape(b * s, WIDTH_B),
                         g_in, w2, wa, wb, wo, g_fin, 1024).reshape(b, s, d)
    y_sample = _out_proj(xs, oa_s.reshape(ndb, WIDTH_A).astype(BF), ob_s.reshape(ndb, WIDTH_B),
                         g_in, w2, wa, wb, wo, g_fin, ndb).reshape(ndb, 1, d)

    k_s = r[:, WIDTH_A:2 * WIDTH_A].reshape(ndb, N_HEADS_A, 1, HEAD_DIM_A)
    v_s = r[:, 2 * WIDTH_A:3 * WIDTH_A].reshape(ndb, N_HEADS_A, 1, HEAD_DIM_A)
    k_p = jnp.swapaxes(k_pg, -1, -2)
    v_p = jnp.swapaxes(v_pg, -1, -2)
    return (y_prompt, y_sample, k_p[None], v_p[None], s_prompt[None], k_s[None], v_s[None], s_sample[None])
```

```python
import functools

import numpy as np
import jax
import jax.numpy as jnp
from jax import lax
from jax.experimental import pallas as pl
from jax.experimental.pallas import tpu as pltpu

N_HEADS_A = 8
HEAD_DIM_A = 64
WIDTH_A = N_HEADS_A * HEAD_DIM_A
MOBA_BLOCK = 256
MOBA_TOPK = 3
N_HEADS_B = 4
KEY_DIM_B = 128
VAL_DIM_B = 128
WIDTH_B = N_HEADS_B * VAL_DIM_B
HGRN_CHUNK = 32
NORM_EPS = 1e-6

LANES = 128
MASK_LANES = LANES
MAX_BLOCKS = MASK_LANES // N_HEADS_A
NEG = -(2.0 ** 30)
SLOPE_LANE = HEAD_DIM_A
SLOPE_PARTS = 3
LOG2E = 1.4426950408889634
BF16_ROWS = 16
V_ROWS = HEAD_DIM_A + BF16_ROWS
MOBA_HEADS = 4
MOBA_GROUP = (8, 4, 2, 1)
VMEM_LIMIT = 48 * 1024 * 1024

F32 = jnp.float32
BF = jnp.bfloat16
NT = (((1,), (1,)), ((), ()))
TN = (((0,), (0,)), ((), ()))


def _alibi_slopes(n):
    return [2.0 ** (-8.0 * (i + 1) / n) for i in range(n)]


def _rms(x, g):
    ms = jnp.mean(x * x, axis=-1, keepdims=True)
    return x * lax.rsqrt(ms + NORM_EPS) * g


def _split2(a):
    hi = a.astype(BF)
    lo = (a - hi.astype(F32)).astype(BF)
    return hi, lo


def _split3(a):
    hi = a.astype(BF)
    r = a - hi.astype(F32)
    mid = r.astype(BF)
    lo = (r - mid.astype(F32)).astype(BF)
    return hi, mid, lo


def _dot(a, b):
    return jnp.dot(a, b, preferred_element_type=F32)


def _dot_nt(a, b):
    return lax.dot_general(a, b, NT, preferred_element_type=F32)


def _dot3_nt(a, b):
    ah, al = _split2(a)
    bh, bl = _split2(b)
    return _dot_nt(ah, bh) + _dot_nt(ah, bl) + _dot_nt(al, bh)


def _sigmoid(x):
    return 1.0 / (1.0 + jnp.exp(-x))


def _silu(x):
    return x * _sigmoid(x)


def _params(*sem):
    return pltpu.CompilerParams(dimension_semantics=sem, vmem_limit_bytes=VMEM_LIMIT)


def _topk_blocks(gate, valid):
    x = jnp.where(valid, gate, -jnp.inf)
    blk1 = lax.broadcasted_iota(jnp.int32, (1, MASK_LANES), 1) // N_HEADS_A
    rank = jnp.zeros(gate.shape, F32)
    for kb in range(1, MAX_BLOCKS):
        y = pltpu.roll(x, kb * N_HEADS_A, axis=1)
        lower_index = jnp.where(blk1 >= kb, 1.0, 0.0)
        rank = rank + jnp.where(y > x, 1.0, jnp.where(y == x, lower_index, 0.0))
    return jnp.where(valid, jnp.where(rank < MOBA_TOPK, 1.0, 0.0), 0.0)


def _proj_prompt_kernel(x_ref, g_ref, w_ref, qext_ref, qh_ref, kh_ref, mb_ref, vt_ref,
                        kpg_ref, vpg_ref, qb_ref, fb_ref, ib_ref, km_ref):
    i = pl.program_id(1)
    t = x_ref.shape[1]
    page = kpg_ref.shape[4]
    @pl.when(i == 0)
    def _():
        km_ref[...] = jnp.zeros(km_ref.shape, F32)

    hn = _rms(x_ref[0], g_ref[...]).astype(BF)

    def proj(c):
        return _dot(hn, w_ref[:, c * WIDTH_A:(c + 1) * WIDTH_A])

    lane = lax.broadcasted_iota(jnp.int32, (t, LANES), 1)
    row = lax.broadcasted_iota(jnp.int32, (t, LANES), 0)
    q = proj(0)
    gate = _dot3_nt(q, km_ref[...])
    blk = lane // N_HEADS_A
    sel = _topk_blocks(gate, blk < i)
    mb_ref[0] = jnp.where(sel > 0.0, 0.0, NEG).astype(BF)

    k = proj(1)
    kmean = jnp.sum(k, axis=0, keepdims=True) / t
    hrow = lax.broadcasted_iota(jnp.int32, (N_HEADS_A, WIDTH_A), 0)
    hcol = lax.broadcasted_iota(jnp.int32, (N_HEADS_A, WIDTH_A), 1) // HEAD_DIM_A
    km_ref[pl.ds(pl.multiple_of(i * N_HEADS_A, N_HEADS_A), N_HEADS_A), :] = jnp.where(
        hrow == hcol, jnp.broadcast_to(kmean, (N_HEADS_A, WIDTH_A)), 0.0)

    low = lane < HEAD_DIM_A
    kext = jnp.where((lane >= SLOPE_LANE) & (lane < SLOPE_LANE + SLOPE_PARTS), row.astype(F32), 0.0)
    scale = HEAD_DIM_A ** -0.5 * LOG2E
    for hp in range(N_HEADS_A // 2):
        cols = slice(hp * LANES, (hp + 1) * LANES)
        qp = q[:, cols] * scale
        kp = k[:, cols]
        qpr = pltpu.roll(qp, HEAD_DIM_A, axis=1)
        kpr = pltpu.roll(kp, HEAD_DIM_A, axis=1)
        qh_ref[0, 2 * hp] = jnp.where(low, qp, qext_ref[2 * hp:2 * hp + 1, :]).astype(BF)
        qh_ref[0, 2 * hp + 1] = jnp.where(low, qpr, qext_ref[2 * hp + 1:2 * hp + 2, :]).astype(BF)
        kh_ref[0, 2 * hp, 0] = jnp.where(low, kp, kext).astype(BF)
        kh_ref[0, 2 * hp + 1, 0] = jnp.where(low, kpr, kext).astype(BF)

    v = proj(2)
    kt = k.T
    vt = v.T
    ones_row = (lax.broadcasted_iota(jnp.int32, (BF16_ROWS, t), 0) == 0).astype(F32)
    for h in range(N_HEADS_A):
        feat = slice(h * HEAD_DIM_A, (h + 1) * HEAD_DIM_A)
        vt_ref[0, h, 0] = jnp.concatenate([vt[feat, :], ones_row], axis=0).astype(BF)
        for p in range(t // page):
            toks = slice(p * page, (p + 1) * page)
            kpg_ref[0, p, h] = kt[feat, toks]
            vpg_ref[0, p, h] = vt[feat, toks]

    qb_ref[0] = proj(3)
    fb_ref[0] = proj(4)
    ib_ref[0] = proj(5)


def _proj_prompt(x, g_in, w1, qext, page):
    b, s, d = x.shape
    t = MOBA_BLOCK
    nblk = s // t
    nproj = w1.shape[1]
    out_shape = (
        jax.ShapeDtypeStruct((b, N_HEADS_A, s, LANES), BF),
        jax.ShapeDtypeStruct((b, N_HEADS_A, nblk, t, LANES), BF),
        jax.ShapeDtypeStruct((b, s, MASK_LANES), BF),
        jax.ShapeDtypeStruct((b, N_HEADS_A, nblk, V_ROWS, t), BF),
        jax.ShapeDtypeStruct((b, s // page, N_HEADS_A, HEAD_DIM_A, page), F32),
        jax.ShapeDtypeStruct((b, s // page, N_HEADS_A, HEAD_DIM_A, page), F32),
        jax.ShapeDtypeStruct((b, s, WIDTH_B), F32),
        jax.ShapeDtypeStruct((b, s, WIDTH_B), F32),
        jax.ShapeDtypeStruct((b, s, WIDTH_B), F32),
    )
    pg = pl.BlockSpec((1, t // page, N_HEADS_A, HEAD_DIM_A, page), lambda bi, i: (bi, i, 0, 0, 0))
    row3 = lambda w: pl.BlockSpec((1, t, w), lambda bi, i: (bi, i, 0))
    return pl.pallas_call(
        _proj_prompt_kernel,
        grid=(b, nblk),
        in_specs=[
            row3(d),
            pl.BlockSpec((1, d), lambda bi, i: (0, 0)),
            pl.BlockSpec((d, nproj), lambda bi, i: (0, 0)),
            pl.BlockSpec((N_HEADS_A, LANES), lambda bi, i: (0, 0)),
        ],
        out_specs=(
            pl.BlockSpec((1, N_HEADS_A, t, LANES), lambda bi, i: (bi, 0, i, 0)),
            pl.BlockSpec((1, N_HEADS_A, 1, t, LANES), lambda bi, i: (bi, 0, i, 0, 0)),
            row3(MASK_LANES),
            pl.BlockSpec((1, N_HEADS_A, 1, V_ROWS, t), lambda bi, i: (bi, 0, i, 0, 0)),
            pg, pg, row3(WIDTH_B), row3(WIDTH_B), row3(WIDTH_B),
        ),
        out_shape=out_shape,
        scratch_shapes=[pltpu.VMEM((MASK_LANES, WIDTH_A), F32)],
        compiler_params=_params("arbitrary", "arbitrary"),
        name="proj_prompt",
    )(x, g_in, w1, qext)


def _moba_kernel(slopes_ref, pt_ref, qh_ref, mb_ref, kh_ref, vt_ref, oh_ref, ck_ref, o_ref, bs_ref, pbuf, psem):
    hg = pl.program_id(1)
    qi = pl.program_id(2)
    step = (pl.program_id(0) * pl.num_programs(1) + hg) * pl.num_programs(2) + qi
    nsteps = pl.num_programs(0) * pl.num_programs(1) * pl.num_programs(2)
    _page_sums(step, nsteps, pt_ref, ck_ref, bs_ref, pbuf, psem)

    t = mb_ref.shape[1]
    mb = mb_ref[0]
    krow = lax.broadcasted_iota(jnp.int32, (t, t), 0)
    qcol = lax.broadcasted_iota(jnp.int32, (t, t), 1)
    causal = krow <= qcol
    heads = range(qh_ref.shape[1])
    q_aug = [jnp.concatenate([qh_ref[0, hh], mb], axis=1) for hh in heads]
    slope = [slopes_ref[hg * len(heads) + hh] for hh in heads]
    no_mask = jnp.zeros((t, MASK_LANES), BF)

    s_own = [_dot_nt(jnp.concatenate([kh_ref[0, hh, qi], no_mask], axis=1), q_aug[hh]) for hh in heads]
    carry = []
    for hh in heads:
        s = jnp.where(causal, s_own[hh], NEG)
        m = jnp.max(s, axis=0, keepdims=True)
        carry.append((m, jnp.exp2(s - m).astype(BF)))
    carry = tuple((m, _dot(vt_ref[0, hh, qi], p)) for hh, (m, p) in zip(heads, carry))

    def past_group(grp, first, g, carry):
        j0 = first + g * grp
        s_all = []
        for hh in heads:
            k_aug = jnp.concatenate([kh_ref[0, hh, pl.ds(j0, grp)].reshape(grp * t, LANES),
                                     oh_ref[hh, pl.ds(j0, grp)].reshape(grp * t, MASK_LANES)], axis=1)
            s_all.append(_dot_nt(k_aug, q_aug[hh]))
        stats = []
        for hh in heads:
            m, _ = carry[hh]
            subs = [s_all[hh][i * t:(i + 1) * t] for i in range(grp)]
            cj = [slope[hh] * ((j0 + i - qi) * t).astype(F32) for i in range(grp)]
            m_new = m
            for i in range(grp):
                m_new = jnp.maximum(m_new, jnp.max(subs[i], axis=0, keepdims=True) + cj[i])
            p = jnp.concatenate([jnp.exp2(subs[i] - (m_new - cj[i])).astype(BF) for i in range(grp)], axis=0)
            stats.append((m_new, jnp.exp2(m - m_new), p))
        out = []
        for hh in heads:
            m_new, alpha, p = stats[hh]
            vt = jnp.concatenate([vt_ref[0, hh, j0 + i] for i in range(grp)], axis=1)
            out.append((m_new, alpha * carry[hh][1] + _dot(vt, p)))
        return tuple(out)

    first = 0
    for grp in MOBA_GROUP:
        n = (qi - first) // grp
        carry = lax.fori_loop(0, n, functools.partial(past_group, grp, first), carry)
        first = first + n * grp
    o_ref[0] = jnp.concatenate(
        [(acc[:HEAD_DIM_A] / acc[HEAD_DIM_A:HEAD_DIM_A + 1]).T for (_, acc) in carry], axis=1).astype(o_ref.dtype)


def _page_sums(step, nsteps, pt_ref, ck_ref, bs_ref, pbuf, psem):
    pps = pbuf.shape[1]
    half = pps // 2

    def copy(s, slot, i):
        page = pt_ref[s * pps + (2 * i if i < half else 2 * (i - half) + 1)]
        return pltpu.make_async_copy(ck_ref.at[page], pbuf.at[slot, i], psem.at[slot])

    slot = step % 2

    @pl.when(step == 0)
    def _():
        for i in range(pps):
            copy(0, 0, i).start()

    @pl.when(step + 1 < nsteps)
    def _():
        for i in range(pps):
            copy(step + 1, 1 - slot, i).start()

    base = (step * half) % LANES

    @pl.when(base == 0)
    def _():
        bs_ref[...] = jnp.zeros(bs_ref.shape, F32)

    for i in range(pps):
        copy(step, slot, i).wait()
    lane = lax.broadcasted_iota(jnp.int32, (HEAD_DIM_A, LANES), 1)
    for h in range(N_HEADS_A):
        feat = slice(h * HEAD_DIM_A, (h + 1) * HEAD_DIM_A)
        tile = bs_ref[0, feat, :]
        for i in range(half):
            red = jnp.sum(pbuf[slot, i, h] + pbuf[slot, half + i, h], axis=1, keepdims=True)
            tile = jnp.where(lane == base + i, red, tile)
        bs_ref[0, feat, :] = tile


def _moba_prompt(slopes, pt_flat, qh, mb, kh, vt, onehot, cache_kt):
    b, _, s, _ = qh.shape
    t = MOBA_BLOCK
    nblk = s // t
    hg = MOBA_HEADS
    vrows = vt.shape[3]
    _, h, dh, page = cache_kt.shape
    ngrp = N_HEADS_A // hg
    nsteps = b * ngrp * nblk
    pps = pt_flat.shape[0] // nsteps
    half = pps // 2
    assert pps * nsteps == pt_flat.shape[0] and pps % 2 == 0 and LANES % half == 0
    bs_shape = (pl.cdiv(nsteps * half, LANES), WIDTH_A, LANES)
    return pl.pallas_call(
        _moba_kernel,
        grid_spec=pltpu.PrefetchScalarGridSpec(
            num_scalar_prefetch=2,
            grid=(b, ngrp, nblk),
            in_specs=[
                pl.BlockSpec((1, hg, t, LANES), lambda bi, g, qi, sl, pt: (bi, g, qi, 0)),
                pl.BlockSpec((1, t, MASK_LANES), lambda bi, g, qi, sl, pt: (bi, qi, 0)),
                pl.BlockSpec((1, hg, nblk, t, LANES), lambda bi, g, qi, sl, pt: (bi, g, 0, 0, 0)),
                pl.BlockSpec((1, hg, nblk, vrows, t), lambda bi, g, qi, sl, pt: (bi, g, 0, 0, 0)),
                pl.BlockSpec((hg, nblk, t, MASK_LANES), lambda bi, g, qi, sl, pt: (g, 0, 0, 0)),
                pl.BlockSpec(memory_space=pl.ANY),
            ],
            out_specs=(
                pl.BlockSpec((1, t, hg * HEAD_DIM_A), lambda bi, g, qi, sl, pt: (bi, qi, g)),
                pl.BlockSpec((1, WIDTH_A, LANES),
                             lambda bi, g, qi, sl, pt: ((((bi * ngrp + g) * nblk + qi) * half) // LANES, 0, 0)),
            ),
            scratch_shapes=[pltpu.VMEM((2, pps, h, dh, page), F32), pltpu.SemaphoreType.DMA((2,))],
        ),
        out_shape=(jax.ShapeDtypeStruct((b, s, WIDTH_A), BF), jax.ShapeDtypeStruct(bs_shape, F32)),
        compiler_params=_params("arbitrary", "arbitrary", "arbitrary"),
        name="moba_prompt",
    )(slopes, pt_flat, qh, mb, kh, vt, onehot, cache_kt)


def _lower_bound(lb_logits, layer):
    e = jnp.exp(lb_logits - jnp.max(lb_logits, axis=0, keepdims=True))
    return jnp.sum(e[:layer + 1], axis=0, keepdims=True) / jnp.sum(e, axis=0, keepdims=True)


def _hgrn_kernel(layer, sel_ref, pt_ref, lbl_ref, gn_ref, tri_ref, q_ref, f_ref, i_ref,
                 qkv_ref, bias_ref, r_ref, s0_ref, ck_ref, cv_ref,
                 o_ref, sfin_ref, oa_ref, ob_ref, sn_ref, st_ref, kbuf, vbuf, sem):
    ti = pl.program_id(1)
    step = pl.program_id(0) * pl.num_programs(1) + ti
    nsteps = pl.num_programs(0) * pl.num_programs(1)
    _sample_attention(step, nsteps, sel_ref, pt_ref, qkv_ref, bias_ref, ck_ref, cv_ref, oa_ref, kbuf, vbuf, sem)
    _hgrn_sample_step(layer, lbl_ref, gn_ref, r_ref, s0_ref, ob_ref, sn_ref)

    t = q_ref.shape[1]
    c = HGRN_CHUNK
    heads = range(N_HEADS_B)
    hk = lambda a, h: a[:, h * KEY_DIM_B:(h + 1) * KEY_DIM_B]

    @pl.when(ti == 0)
    def _():
        st_ref[...] = jnp.zeros(st_ref.shape, F32)

    lb = _lower_bound(lbl_ref[...], layer)
    f = lb + (1.0 - lb) * _sigmoid(f_ref[0])
    k = 1.0 - f
    tri = tri_ref[...]
    cum = sum(_dot(tri, part) for part in _split3(jnp.log(f)))
    tot = jnp.concatenate(
        [jnp.broadcast_to(cum[(n + 1) * c - 1:(n + 1) * c], (c, cum.shape[1])) for n in range(t // c)], axis=0)
    qd = (q_ref[0] * jnp.exp(cum)).astype(BF)
    ki = (k * jnp.exp(-cum)).astype(BF)
    kd = (k * jnp.exp(tot - cum)).astype(BF)
    vb = i_ref[0].astype(BF)
    a = [jnp.where(tri > 0, _dot_nt(hk(qd, h), hk(ki, h)), 0.0).astype(BF) for h in heads]
    o_intra = [_dot(a[h], hk(vb, h)) for h in heads]
    chunks = [slice(n * c, (n + 1) * c) for n in range(t // c)]
    upd = [[lax.dot_general(hk(vb, h)[r], hk(kd, h)[r], TN, preferred_element_type=F32) for h in heads]
           for r in chunks]
    st = [st_ref[h] for h in heads]
    st_in = []
    for n, r in enumerate(chunks):
        st_in.append([st[h].astype(BF) for h in heads])
        decay = jnp.exp(tot[r.stop - 1:r.stop])
        st = [st[h] * hk(decay, h) + upd[n][h] for h in heads]
    inter = [[_dot_nt(hk(qd, h)[r], st_in[n][h]) for h in heads] for n, r in enumerate(chunks)]
    for h in heads:
        st_ref[h] = st[h]
        gn = gn_ref[:, h * VAL_DIM_B:(h + 1) * VAL_DIM_B]
        o = o_intra[h] + jnp.concatenate([inter[n][h] for n in range(len(chunks))], axis=0)
        o_ref[0, :, h * VAL_DIM_B:(h + 1) * VAL_DIM_B] = _rms(o, gn).astype(o_ref.dtype)

    @pl.when(ti == pl.num_programs(1) - 1)
    def _():
        for h in heads:
            sfin_ref[0, h] = st[h].T


def _chunk_tri(t):
    r = np.arange(t)
    return jnp.asarray((r[:, None] // HGRN_CHUNK == r[None, :] // HGRN_CHUNK) & (r[None, :] <= r[:, None]), BF)


def _hgrn_prompt(layer, lb_logits, gnorm, qb, fb, ib, sel_flat, pt_flat, qkv_rows, bias, r, state, cache_kt, cache_vt):
    b, s, _ = qb.shape
    t = 256
    nt = s // t
    ndb = r.shape[0]
    assert ndb == b * nt, "one sample sequence per hgrn_prompt grid step"
    _, h, dh, page = cache_kt.shape
    ntp = MOBA_TOPK * (MOBA_BLOCK // page)
    tok = pl.BlockSpec((1, t, WIDTH_B), lambda bi, ti, sl, pt: (bi, ti, 0))
    fixed = lambda a: pl.BlockSpec(a.shape, lambda bi, ti, sl, pt: (0,) * a.ndim)
    seq = lambda a: pl.BlockSpec((1,) + a.shape[1:], lambda bi, ti, sl, pt: (bi * nt + ti,) + (0,) * (a.ndim - 1))
    tri = _chunk_tri(t)
    r3 = r.reshape(ndb, 1, r.shape[1])
    oa_shape = jax.ShapeDtypeStruct((ndb, 8, LANES), F32)
    ob_shape = jax.ShapeDtypeStruct((ndb, 1, WIDTH_B), BF)
    sn_shape = jax.ShapeDtypeStruct(state.shape, F32)
    return pl.pallas_call(
        functools.partial(_hgrn_kernel, layer),
        grid_spec=pltpu.PrefetchScalarGridSpec(
            num_scalar_prefetch=2,
            grid=(b, nt),
            in_specs=[
                fixed(lb_logits), fixed(gnorm), fixed(tri), tok, tok, tok,
                seq(qkv_rows), seq(bias), seq(r3), seq(state),
                pl.BlockSpec(memory_space=pl.ANY), pl.BlockSpec(memory_space=pl.ANY),
            ],
            out_specs=(
                tok,
                pl.BlockSpec((1, N_HEADS_B, KEY_DIM_B, VAL_DIM_B), lambda bi, ti, sl, pt: (bi, 0, 0, 0)),
                seq(oa_shape), seq(ob_shape), seq(sn_shape),
            ),
            scratch_shapes=[pltpu.VMEM((N_HEADS_B, VAL_DIM_B, KEY_DIM_B), F32),
                            pltpu.VMEM((2, ntp, h, dh, page), F32),
                            pltpu.VMEM((2, ntp, h, dh, page), F32),
                            pltpu.SemaphoreType.DMA((2, 2))],
        ),
        out_shape=(
            jax.ShapeDtypeStruct((b, s, WIDTH_B), BF),
            jax.ShapeDtypeStruct((b, N_HEADS_B, KEY_DIM_B, VAL_DIM_B), F32),
            oa_shape, ob_shape, sn_shape,
        ),
        compiler_params=_params("arbitrary", "arbitrary"),
        name="hgrn_prompt",
    )(sel_flat, pt_flat, lb_logits, gnorm, tri, qb, fb, ib, qkv_rows, bias, r3, state, cache_kt, cache_vt)


def _out_kernel(x_ref, oa_ref, ob_ref, g_ref, w2_ref, wa_ref, wb_ref, wo_ref, gf_ref, y_ref):
    x = x_ref[...]
    hn = _rms(x, g_ref[...]).astype(BF)
    d = x.shape[1]

    def proj(lo, hi):
        return _dot(hn, w2_ref[:, lo:hi])

    ya = _dot((oa_ref[...].astype(F32) * _silu(proj(0, WIDTH_A))).astype(BF), wa_ref[...])
    yb = _dot((ob_ref[...].astype(F32) * _silu(proj(WIDTH_A, WIDTH_A + WIDTH_B))).astype(BF), wb_ref[...])
    g0 = WIDTH_A + WIDTH_B
    mixed = _sigmoid(proj(g0, g0 + d)) * ya + _sigmoid(proj(g0 + d, g0 + 2 * d)) * yb
    out = x + _dot(mixed.astype(BF), wo_ref[...])
    y_ref[...] = _rms(out, gf_ref[...])


def _out_proj(x2, oa, ob, g_in, w2, wa, wb, wo, g_fin, tm):
    n, d = x2.shape
    rows = lambda w: pl.BlockSpec((tm, w), lambda i: (i, 0))
    full = lambda a: pl.BlockSpec(a.shape, lambda i: (0, 0))
    return pl.pallas_call(
        _out_kernel,
        grid=(n // tm,),
        in_specs=[rows(d), rows(WIDTH_A), rows(WIDTH_B), full(g_in), full(w2), full(wa), full(wb), full(wo),
                  full(g_fin)],
        out_specs=rows(d),
        out_shape=jax.ShapeDtypeStruct((n, d), F32),
        compiler_params=_params("arbitrary"),
        name="out_proj",
    )(x2, oa, ob, g_in, w2, wa, wb, wo, g_fin)


def _proj_sample_kernel(x_ref, g_ref, w_ref, r_ref):
    hn = _rms(x_ref[...], g_ref[...]).astype(BF)
    for c in range(w_ref.shape[1] // WIDTH_A):
        cols = slice(c * WIDTH_A, (c + 1) * WIDTH_A)
        r_ref[:, cols] = _dot(hn, w_ref[:, cols])


def _proj_sample(x2, g_in, w1):
    n, _ = x2.shape
    return pl.pallas_call(
        _proj_sample_kernel,
        out_shape=jax.ShapeDtypeStruct((n, w1.shape[1]), F32),
        compiler_params=pltpu.CompilerParams(vmem_limit_bytes=VMEM_LIMIT),
        name="proj_sample",
    )(x2, g_in, w1)


def _sample_gate_kernel(past, slope_ref, r_ref, bs_ref, sel_ref, bias_ref):
    dpg = r_ref.shape[1]
    nblk = LANES // dpg
    page = bias_ref.shape[3]
    ppb = MOBA_BLOCK // page
    hrow = lax.broadcasted_iota(jnp.int32, (N_HEADS_A, WIDTH_A), 0)
    hcol = lax.broadcasted_iota(jnp.int32, (N_HEADS_A, WIDTH_A), 1) // HEAD_DIM_A
    lane = lax.broadcasted_iota(jnp.int32, (N_HEADS_A, LANES), 1)
    tok = lax.broadcasted_iota(jnp.int32, (N_HEADS_A, page), 1)
    kh, kl = _split2(bs_ref[0] / MOBA_BLOCK)
    for j in range(dpg):
        qbd = jnp.where(hrow == hcol, jnp.broadcast_to(r_ref[0, j:j + 1, :WIDTH_A], (N_HEADS_A, WIDTH_A)), 0.0)
        qh, ql = _split2(qbd)
        gate = _dot(qh, kh) + _dot(qh, kl) + _dot(ql, kh)
        blk = lane - j * nblk
        x = jnp.where((blk >= 0) & (blk < nblk), gate, -jnp.inf)
        out = jnp.zeros((N_HEADS_A, LANES), jnp.int32)
        for rank in range(MOBA_TOPK):
            m = jnp.max(x, axis=1, keepdims=True)
            first = jnp.min(jnp.where(x == m, blk, LANES), axis=1, keepdims=True)
            out = jnp.where(lane == rank, first, out)
            x = jnp.where(blk == first, -jnp.inf, x)
            for p in range(ppb):
                dist = past - (first * MOBA_BLOCK + p * page) - tok
                bias_ref[j, rank * ppb + p] = -slope_ref[...] * dist.astype(F32)
        for unused in range(MOBA_TOPK * ppb, bias_ref.shape[1]):
            bias_ref[j, unused] = jnp.zeros((N_HEADS_A, page), F32)
        sel_ref[j] = out


def _sample_gate(past, r, bsum_t, nblk, page):
    ndb = r.shape[0]
    dpg = LANES // nblk
    slots = pl.cdiv(MOBA_TOPK * (MOBA_BLOCK // page), 8) * 8
    slope_col = jnp.asarray(_alibi_slopes(N_HEADS_A), F32)[:, None]
    return pl.pallas_call(
        functools.partial(_sample_gate_kernel, past),
        grid=(ndb // dpg,),
        in_specs=[pl.BlockSpec((N_HEADS_A, 1), lambda i: (0, 0)),
                  pl.BlockSpec((1, dpg, r.shape[1]), lambda i: (i, 0, 0)),
                  pl.BlockSpec((1, WIDTH_A, LANES), lambda i: (i, 0, 0))],
        out_specs=(pl.BlockSpec((dpg, N_HEADS_A, LANES), lambda i: (i, 0, 0)),
                   pl.BlockSpec((dpg, slots, N_HEADS_A, page), lambda i: (i, 0, 0, 0))),
        out_shape=(jax.ShapeDtypeStruct((ndb, N_HEADS_A, LANES), jnp.int32),
                   jax.ShapeDtypeStruct((ndb, slots, N_HEADS_A, page), F32)),
        compiler_params=_params("arbitrary"),
        name="sample_gate",
    )(slope_col, r.reshape(ndb // dpg, dpg, r.shape[1]), bsum_t)


def _sample_attention(db, ndb, sel_ref, pt_ref, qkv_ref, bias_ref, ck_ref, cv_ref, o_ref, kbuf, vbuf, sem):
    npg = pt_ref.shape[0] // ndb
    ntp = kbuf.shape[1]
    ppb = ntp // MOBA_TOPK

    def copies(d, slot):
        out = []
        for h in range(N_HEADS_A):
            for tp in range(ntp):
                blk = sel_ref[(d * N_HEADS_A + h) * MOBA_TOPK + tp // ppb]
                pg = pt_ref[d * npg + blk * ppb + tp % ppb]
                out.append(pltpu.make_async_copy(ck_ref.at[pg, h], kbuf.at[slot, tp, h], sem.at[0, slot]))
                out.append(pltpu.make_async_copy(cv_ref.at[pg, h], vbuf.at[slot, tp, h], sem.at[1, slot]))
        return out

    slot = db % 2

    @pl.when(db == 0)
    def _():
        for cp in copies(0, 0):
            cp.start()

    @pl.when(db + 1 < ndb)
    def _():
        for cp in copies(db + 1, 1 - slot):
            cp.start()

    for cp in copies(db, slot):
        cp.wait()

    rows = qkv_ref.shape[1]
    xt = jnp.concatenate([qkv_ref[0], jnp.zeros((LANES - rows, LANES), F32)], axis=0).T
    hpc = LANES // HEAD_DIM_A

    def head_columns(first_col):
        return jnp.stack([xt[(h % hpc) * HEAD_DIM_A:(h % hpc + 1) * HEAD_DIM_A,
                             first_col + h // hpc:first_col + h // hpc + 1] for h in range(N_HEADS_A)])

    ncol = WIDTH_A // LANES
    qc = head_columns(0) * HEAD_DIM_A ** -0.5
    knc = head_columns(ncol)
    vnc = head_columns(2 * ncol)
    lg_new = jnp.sum(qc * knc, axis=1)
    lg = jnp.sum(kbuf[slot] * qc[None], axis=2) + bias_ref[0, :ntp]
    m = jnp.maximum(jnp.max(jnp.max(lg, axis=0), axis=1, keepdims=True), lg_new)
    p_new = jnp.exp(lg_new - m)
    p = jnp.exp(lg - m[None])
    l = jnp.sum(jnp.sum(p, axis=0), axis=1, keepdims=True) + p_new
    pv = jnp.sum(vbuf[slot] * p[:, :, None, :], axis=0)
    acc = jnp.sum(pv, axis=2, keepdims=True) + p_new[:, :, None] * vnc
    o = (acc / l[:, :, None]).reshape(ncol, LANES, 1)
    lane = lax.broadcasted_iota(jnp.int32, (LANES, LANES), 1)
    cols = jnp.zeros((LANES, LANES), F32)
    for c in range(ncol):
        cols = jnp.where(lane == c, jnp.broadcast_to(o[c], (LANES, LANES)), cols)
    o_ref[0] = cols.T[:o_ref.shape[1]]


def _hgrn_sample_step(layer, lbl_ref, gn_ref, r_ref, s_ref, o_ref, sn_ref):
    ndb = r_ref.shape[0]
    kk = KEY_DIM_B
    eye = (lax.broadcasted_iota(jnp.int32, (kk, kk), 0) == lax.broadcasted_iota(jnp.int32, (kk, kk), 1))
    ones = jnp.ones((kk, VAL_DIM_B), BF)

    def column(vec):
        diag = jnp.where(eye, jnp.broadcast_to(vec, (kk, kk)), 0.0)
        return sum(_dot(part, ones) for part in _split3(diag))

    lb_all = _lower_bound(lbl_ref[...], layer)
    q0, f0, i0 = WIDTH_A * 3, WIDTH_A * 3 + WIDTH_B, WIDTH_A * 3 + 2 * WIDTH_B
    for db in range(ndb):
        row = r_ref[db]
        outs = []
        for h in range(N_HEADS_B):
            cols = slice(h * kk, (h + 1) * kk)
            lb = lb_all[:, cols]
            q = row[:, q0 + h * kk:q0 + (h + 1) * kk]
            f = lb + (1.0 - lb) * _sigmoid(row[:, f0 + h * kk:f0 + (h + 1) * kk])
            v = row[:, i0 + h * VAL_DIM_B:i0 + (h + 1) * VAL_DIM_B]
            fcol = column(f)
            s_new = fcol * s_ref[db, h] + (1.0 - fcol) * v
            sn_ref[db, h] = s_new
            o = jnp.sum(column(q) * s_new, axis=0, keepdims=True)
            outs.append(_rms(o, gn_ref[:, h * VAL_DIM_B:(h + 1) * VAL_DIM_B]))
        o_ref[db] = jnp.concatenate(outs, axis=1).astype(o_ref.dtype)


def _q_extension():
    ext = np.zeros((N_HEADS_A, LANES), np.float32)
    for h, slope in enumerate(_alibi_slopes(N_HEADS_A)):
        rest = np.float32(slope * LOG2E)
        for part in range(SLOPE_PARTS):
            piece = np.float32(rest.astype(jnp.bfloat16))
            ext[h, SLOPE_LANE + part] = piece
            rest = np.float32(rest - piece)
    return jnp.asarray(ext)


def _block_onehot(nblk, t):
    lane = np.arange(MASK_LANES)[None, None, :]
    tgt = (np.arange(nblk)[None, :, None] * N_HEADS_A + np.arange(N_HEADS_A)[:, None, None])
    oh = (lane == tgt).astype(np.float32)
    return jnp.broadcast_to(jnp.asarray(oh, BF)[:, :, None, :], (N_HEADS_A, nblk, t, MASK_LANES))


def kernel(x_prompt, x_sample, cache_k, cache_v, state_hgrn, page_table, norm_in_g, w_in,
           hgrn_lb_logits, hgrn_norm_g, w_branch_a, w_branch_b, w_out, final_norm_g):
    b, s, d = x_prompt.shape
    ndb, ds, _ = x_sample.shape
    depth, n_pool, _, page, _ = cache_k.shape
    npg = page_table.shape[1]
    past = npg * page
    ppb = MOBA_BLOCK // page
    assert depth == 1 and ds == 1
    assert s % (MOBA_BLOCK * min(MOBA_GROUP)) == 0 and s // MOBA_BLOCK <= MAX_BLOCKS
    assert past % MOBA_BLOCK == 0 and past // MOBA_BLOCK >= MOBA_TOPK
    assert ppb == 2
    nblk_s = npg // ppb
    assert LANES % nblk_s == 0 and ndb % (LANES // nblk_s) == 0
    layer = 0

    wk = N_HEADS_B * KEY_DIM_B
    a3 = 3 * WIDTH_A
    wl = w_in[layer]
    w1 = jnp.concatenate([wl[:, :a3], wl[:, a3 + WIDTH_A:a3 + WIDTH_A + 2 * wk + WIDTH_B]], axis=1).astype(BF)
    zb0 = a3 + WIDTH_A + 2 * wk + WIDTH_B
    w2 = jnp.concatenate([wl[:, a3:a3 + WIDTH_A], wl[:, zb0:]], axis=1).astype(BF)
    wa = w_branch_a[layer].astype(BF)
    wb = w_branch_b[layer].astype(BF)
    wo = w_out[layer].astype(BF)
    g_in = norm_in_g[layer][None, :]
    g_fin = final_norm_g[None, :]
    gnorm = hgrn_norm_g[layer][None, :]
    slopes = jnp.asarray([m * LOG2E for m in _alibi_slopes(N_HEADS_A)], F32)

    pt_flat = page_table.reshape(ndb * npg)
    ckt = jnp.swapaxes(cache_k[layer], -1, -2)
    cvt = jnp.swapaxes(cache_v[layer], -1, -2)

    xs = x_sample.reshape(ndb, d)
    r = _proj_sample(xs, g_in, w1)
    qh, kh, mb, vt, k_pg, v_pg, qb, fb, ib = _proj_prompt(x_prompt, g_in, w1, _q_extension(), page)
    o_a, bsum_t = _moba_prompt(slopes, pt_flat, qh, mb, kh, vt, _block_onehot(s // MOBA_BLOCK, MOBA_BLOCK), ckt)
    sel, bias = _sample_gate(past, r, bsum_t, nblk_s, page)
    sel = sel[:, :, :MOBA_TOPK].reshape(ndb * N_HEADS_A * MOBA_TOPK)
    qkv_rows = r[:, :3 * WIDTH_A].reshape(ndb, 3 * WIDTH_A // LANES, LANES)
    o_b, s_prompt, oa_s, ob_s, s_sample = _hgrn_prompt(
        layer, hgrn_lb_logits, gnorm, qb, fb, ib, sel, pt_flat, qkv_rows, bias, r, state_hgrn[layer], ckt, cvt)
    oa_s = oa_s[:, :WIDTH_A // LANES]
    y_prompt = _out_proj(x_prompt.reshape(b * s, d), o_a.reshape(b * s, WIDTH_A), o_b.reshape(b * s, WIDTH_B),
                         g_in, w2, wa, wb, wo, g_fin, 1024).reshape(b, s, d)
    y_sample = _out_proj(xs, oa_s.reshape(ndb, WIDTH_A).astype(BF), ob_s.reshape(ndb, WIDTH_B),
                         g_in, w2, wa, wb, wo, g_fin, ndb).reshape(ndb, 1, d)

    k_s = r[:, WIDTH_A:2 * WIDTH_A].reshape(ndb, N_HEADS_A, 1, HEAD_DIM_A)
    v_s = r[:, 2 * WIDTH_A:3 * WIDTH_A].reshape(ndb, N_HEADS_A, 1, HEAD_DIM_A)
    k_p = jnp.swapaxes(k_pg, -1, -2)
    v_p = jnp.swapaxes(v_pg, -1, -2)
    return (y_prompt, y_sample, k_p[None], v_p[None], s_prompt[None], k_s[None], v_s[None], s_sample[None])
```
